```python
import math
import jax, jax.numpy as jnp
from jax import lax
import numpy as np

D_MODEL = 1024
BATCH = 8
SEQ = 2048
DEPTH = 4

N_EVEN = (DEPTH + 1) // 2
N_ODD = DEPTH // 2
EPS = 1e-6

DA_HEADS = 4
DA_HEAD_DIM = 64
DA_V_DIM = 2 * DA_HEAD_DIM
DA_QK_WIDTH = DA_HEADS * 2 * DA_HEAD_DIM
DA_V_WIDTH = DA_HEADS * DA_V_DIM
ROPE_THETA = 10000.0
Q_BLOCK = 128

SG_GROUPS = 4
SG_CHUNK = 128
SG_GROUP_DIM = 128
SG_WIDTH = SG_GROUPS * SG_GROUP_DIM

AB_IN_WIDTH = 2 * DA_QK_WIDTH + DA_V_WIDTH + 2 * SG_WIDTH
AB_OUT_WIDTH = DA_V_WIDTH + SG_WIDTH

CONV_INNER = D_MODEL
CONV_WIDTH = 31

FFN_HIDDEN = 2816
FFN_CONV_WIDTH = 3

kernel_name = "hybrid_diffattn_sgmlp_conformer_convffn"


def rms_norm(x, g):
    x32 = x.astype(jnp.float32)
    y = x32 * lax.rsqrt(jnp.mean(x32 * x32, axis=-1, keepdims=True) + EPS)
    return (y * g.astype(jnp.float32)).astype(x.dtype)


def layer_norm(x, g, b):
    x32 = x.astype(jnp.float32)
    mu = jnp.mean(x32, axis=-1, keepdims=True)
    xc = x32 - mu
    y = xc * lax.rsqrt(jnp.mean(xc * xc, axis=-1, keepdims=True) + EPS)
    return (y * g.astype(jnp.float32) + b.astype(jnp.float32)).astype(x.dtype)


def causal_dwconv(x, w, b):
    k = w.shape[0]
    y = lax.conv_general_dilated(
        x, w[:, None, :].astype(x.dtype), window_strides=(1,),
        padding=[(k - 1, 0)], dimension_numbers=("NWC", "WIO", "NWC"),
        feature_group_count=x.shape[-1])
    return y + b.astype(x.dtype)


def rope_tables(seq, dim):
    inv = 1.0 / (ROPE_THETA ** (jnp.arange(0, dim, 2, dtype=jnp.float32) / dim))
    ang = jnp.arange(seq, dtype=jnp.float32)[:, None] * inv[None, :]
    ang = jnp.concatenate([ang, ang], axis=-1)
    return jnp.cos(ang), jnp.sin(ang)


def apply_rope(x, cos, sin):
    x32 = x.astype(jnp.float32)
    x1, x2 = jnp.split(x32, 2, axis=-1)
    rot = jnp.concatenate([-x2, x1], axis=-1)
    c = cos[None, :, None, None, :]
    s = sin[None, :, None, None, :]
    return (x32 * c + rot * s).astype(x.dtype)


def diff_attention(q, k, v, lam):
    b, s, h, _, d = q.shape
    nb = s // Q_BLOCK
    scale = d ** -0.5
    qb = q.reshape(b, nb, Q_BLOCK, h, 2, d).transpose(1, 0, 2, 3, 4, 5)
    kpos = jnp.arange(s)

    def block(args):
        q_blk, start = args
        sc = jnp.einsum("bqhcd,bkhcd->bhcqk", q_blk, k,
                        preferred_element_type=jnp.float32) * scale
        qpos = start + jnp.arange(Q_BLOCK)
        mask = kpos[None, :] <= qpos[:, None]
        sc = jnp.where(mask, sc, -jnp.inf)
        p = jax.nn.softmax(sc, axis=-1)
        a = p[:, :, 0] - lam * p[:, :, 1]
        return jnp.einsum("bhqk,bkhe->bqhe", a.astype(v.dtype), v)

    out = lax.map(block, (qb, jnp.arange(nb) * Q_BLOCK))
    return out.transpose(1, 0, 2, 3, 4).reshape(b, s, h, v.shape[-1])


def spatial_gating(u, z, ln_g, ln_b, w_s, b_s):
    b, s, _ = u.shape
    nc = s // SG_CHUNK
    z = z.reshape(b, s, SG_GROUPS, SG_GROUP_DIM)
    z = layer_norm(z, ln_g.reshape(SG_GROUPS, SG_GROUP_DIM),
                   ln_b.reshape(SG_GROUPS, SG_GROUP_DIM))
    z = z.reshape(b, nc, SG_CHUNK, SG_GROUPS, SG_GROUP_DIM)
    tri = jnp.tril(jnp.ones((SG_CHUNK, SG_CHUNK), dtype=bool))
    w = jnp.where(tri[None], w_s, 0).astype(z.dtype)
    zs = jnp.einsum("gts,bcsgd->bctgd", w, z) \
        + b_s.T.astype(z.dtype)[None, None, :, :, None]
    return u * zs.reshape(b, s, SG_WIDTH)


def even_mixer(h, layer_idx, w_in, w_out, lq1, lk1, lq2, lk2, subln_g,
               sg_ln_g, sg_ln_b, sg_w, sg_b, cos, sin):
    b, s, _ = h.shape
    proj = h @ w_in
    o0 = DA_QK_WIDTH
    o1 = 2 * DA_QK_WIDTH
    o2 = o1 + DA_V_WIDTH
    q = proj[..., :o0].reshape(b, s, DA_HEADS, 2, DA_HEAD_DIM)
    k = proj[..., o0:o1].reshape(b, s, DA_HEADS, 2, DA_HEAD_DIM)
    v = proj[..., o1:o2].reshape(b, s, DA_HEADS, DA_V_DIM)
    uz = jax.nn.gelu(proj[..., o2:])
    u, z = jnp.split(uz, 2, axis=-1)

    q = apply_rope(q, cos, sin)
    k = apply_rope(k, cos, sin)
    lambda_init = 0.8 - 0.6 * math.exp(-0.3 * layer_idx)
    lam = (jnp.exp(jnp.sum(lq1.astype(jnp.float32) * lk1.astype(jnp.float32)))
           - jnp.exp(jnp.sum(lq2.astype(jnp.float32) * lk2.astype(jnp.float32)))
           + lambda_init)
    o = diff_attention(q, k, v, lam)
    o = rms_norm(o, subln_g) * (1.0 - lambda_init)
    o = o.reshape(b, s, DA_V_WIDTH)

    g = spatial_gating(u, z, sg_ln_g, sg_ln_b, sg_w, sg_b)
    return jnp.concatenate([o, g], axis=-1) @ w_out


def odd_mixer(h, w_in, b_in, dw_w, dw_b, ln_g, ln_b, w_out, b_out):
    a = h @ w_in + b_in
    a, gate = jnp.split(a, 2, axis=-1)
    y = a * jax.nn.sigmoid(gate)
    y = causal_dwconv(y, dw_w, dw_b)
    y = jax.nn.silu(layer_norm(y, ln_g, ln_b))
    return y @ w_out + b_out


def conv_ffn(h, w_up, dw_w, dw_b, w_down):
    hh = causal_dwconv(h @ w_up, dw_w, dw_b)
    a, gate = jnp.split(hh, 2, axis=-1)
    return (a * jax.nn.silu(gate)) @ w_down


def setup_inputs(seed: int = 0) -> dict:
    key = jax.random.key(seed)
    ks = iter(jax.random.split(key, 40))
    f32 = jnp.float32

    def nrm(shape, scale):
        return jax.random.normal(next(ks), shape, f32) * scale

    def gain(shape):
        return 1.0 + nrm(shape, 0.02)

    d = D_MODEL
    return {
        "x": nrm((BATCH, SEQ, d), 1.0),
        "norm_mix_g": gain((DEPTH, d)),
        "norm_ffn_g": gain((DEPTH, d)),
        "ab_w_in": nrm((N_EVEN, d, AB_IN_WIDTH), d ** -0.5),
        "ab_w_out": nrm((N_EVEN, AB_OUT_WIDTH, d), AB_OUT_WIDTH ** -0.5),
        "diff_lq1": nrm((N_EVEN, DA_HEAD_DIM), 0.1),
        "diff_lk1": nrm((N_EVEN, DA_HEAD_DIM), 0.1),
        "diff_lq2": nrm((N_EVEN, DA_HEAD_DIM), 0.1),
        "diff_lk2": nrm((N_EVEN, DA_HEAD_DIM), 0.1),
        "diff_subln_g": gain((N_EVEN, DA_V_DIM)),
        "sg_ln_g": gain((N_EVEN, SG_WIDTH)),
        "sg_ln_b": nrm((N_EVEN, SG_WIDTH), 0.02),
        "sg_w": nrm((N_EVEN, SG_GROUPS, SG_CHUNK, SG_CHUNK), SG_CHUNK ** -0.5),
        "sg_b": gain((N_EVEN, SG_GROUPS, SG_CHUNK)),
        "conv_w_in": nrm((N_ODD, d, 2 * CONV_INNER), d ** -0.5),
        "conv_b_in": nrm((N_ODD, 2 * CONV_INNER), 0.02),
        "conv_dw_w": nrm((N_ODD, CONV_WIDTH, CONV_INNER), CONV_WIDTH ** -0.5),
        "conv_dw_b": nrm((N_ODD, CONV_INNER), 0.02),
        "conv_ln_g": gain((N_ODD, CONV_INNER)),
        "conv_ln_b": nrm((N_ODD, CONV_INNER), 0.02),
        "conv_w_out": nrm((N_ODD, CONV_INNER, d), CONV_INNER ** -0.5),
        "conv_b_out": nrm((N_ODD, d), 0.02),
        "ffn_w_up": nrm((DEPTH, d, 2 * FFN_HIDDEN), d ** -0.5),
        "ffn_dw_w": nrm((DEPTH, FFN_CONV_WIDTH, 2 * FFN_HIDDEN), FFN_CONV_WIDTH ** -0.5),
        "ffn_dw_b": nrm((DEPTH, 2 * FFN_HIDDEN), 0.02),
        "ffn_w_down": nrm((DEPTH, FFN_HIDDEN, d), FFN_HIDDEN ** -0.5),
        "final_norm_g": gain((d,)),
    }


def reference(x, norm_mix_g, norm_ffn_g, ab_w_in, ab_w_out, diff_lq1, diff_lk1,
              diff_lq2, diff_lk2, diff_subln_g, sg_ln_g, sg_ln_b, sg_w, sg_b,
              conv_w_in, conv_b_in, conv_dw_w, conv_dw_b, conv_ln_g, conv_ln_b,
              conv_w_out, conv_b_out, ffn_w_up, ffn_dw_w, ffn_dw_b, ffn_w_down,
              final_norm_g):
    seq = x.shape[1]
    cos, sin = rope_tables(seq, DA_HEAD_DIM)
    for l in range(DEPTH):
        h = rms_norm(x, norm_mix_g[l])
        i = l // 2
        if l % 2 == 0:
            x = x + even_mixer(h, l, ab_w_in[i], ab_w_out[i], diff_lq1[i],
                               diff_lk1[i], diff_lq2[i], diff_lk2[i],
                               diff_subln_g[i], sg_ln_g[i], sg_ln_b[i],
                               sg_w[i], sg_b[i], cos, sin)
        else:
            x = x + odd_mixer(h, conv_w_in[i], conv_b_in[i], conv_dw_w[i],
                              conv_dw_b[i], conv_ln_g[i], conv_ln_b[i],
                              conv_w_out[i], conv_b_out[i])
        x = x + conv_ffn(rms_norm(x, norm_ffn_g[l]), ffn_w_up[l], ffn_dw_w[l],
                         ffn_dw_b[l], ffn_w_down[l])
    return rms_norm(x, final_norm_g)
```

```python
import functools
import math

import jax
import jax.numpy as jnp
from jax import lax
from jax.experimental import pallas as pl
from jax.experimental.pallas import tpu as pltpu

D_MODEL = 1024
DEPTH = 4
EPS = 1e-6

DA_HEADS = 4
DA_HEAD_DIM = 64
DA_V_DIM = 2 * DA_HEAD_DIM
DA_QK_WIDTH = DA_HEADS * 2 * DA_HEAD_DIM
DA_V_WIDTH = DA_HEADS * DA_V_DIM
ROPE_THETA = 10000.0

SG_GROUPS = 4
SG_CHUNK = 128
SG_GROUP_DIM = 128
SG_WIDTH = SG_GROUPS * SG_GROUP_DIM

AB_IN_WIDTH = 2 * DA_QK_WIDTH + DA_V_WIDTH + 2 * SG_WIDTH
AB_OUT_WIDTH = DA_V_WIDTH + SG_WIDTH

CONV_INNER = D_MODEL
CONV_WIDTH = 31
FFN_HIDDEN = 2816
FFN_CONV_WIDTH = 3

LANES = 128
SUBLANES = 8
VMEM_LIMIT = 56 * 1024 * 1024

F32 = jnp.float32
BF16 = jnp.bfloat16

TM_IN = 512
TM_OUT = 512
TM_ODD = 256
TM_FFN = 256
TQ = 256
CONV_HALO = 32
FFN_HALO = SUBLANES


def _rms(x, g):
    return x * lax.rsqrt(jnp.mean(x * x, axis=-1, keepdims=True) + EPS) * g


def _params(n_axes):
    return pltpu.CompilerParams(
        dimension_semantics=("arbitrary",) * n_axes, vmem_limit_bytes=VMEM_LIMIT)


def _const_spec(shape):
    nd = len(shape)
    return pl.BlockSpec(shape, lambda *_: (0,) * nd)


def _ffn_kernel(x_ref, g_ref, wup_ref, dww_ref, dwb_ref, wdown_ref, fg_ref, o_ref,
                hbuf, act, *, tm, final):
    hid = FFN_HIDDEN
    halo = FFN_HALO

    @pl.when(pl.program_id(1) == 0)
    def _():
        hbuf[0:halo, :] = jnp.zeros((halo, 2 * hid), F32)

    x = x_ref[...]
    h = _rms(x, g_ref[...]).astype(BF16)
    nch = 512
    for c in range(0, 2 * hid, nch):
        hbuf[halo:halo + tm, c:c + nch] = jnp.dot(
            h, wup_ref[:, c:c + nch], preferred_element_type=F32)

    cch = 256

    def conv(col):
        w = dww_ref[:, col:col + cch]
        y = dwb_ref[:, col:col + cch] + w[0:1] * hbuf[halo - 2:halo - 2 + tm, col:col + cch]
        y = y + w[1:2] * hbuf[halo - 1:halo - 1 + tm, col:col + cch]
        return y + w[2:3] * hbuf[halo:halo + tm, col:col + cch]

    for c in range(0, hid, cch):
        a = conv(c)
        g = conv(hid + c)
        act[:, c:c + cch] = (a * (g * jax.nn.sigmoid(g))).astype(BF16)

    hbuf[0:halo, :] = hbuf[tm:tm + halo, :]
    y = x + jnp.dot(act[...], wdown_ref[...], preferred_element_type=F32)
    if final:
        y = _rms(y, fg_ref[...])
    o_ref[...] = y


def _conv_ffn(x, g, w_up, dw_w, dw_b, w_down, final_g, final):
    b, s, d = x.shape
    tm = TM_FFN
    hid = FFN_HIDDEN
    return pl.pallas_call(
        functools.partial(_ffn_kernel, tm=tm, final=final),
        grid=(b, s // tm),
        in_specs=[
            pl.BlockSpec((None, tm, d), lambda i, j: (i, j, 0)),
            _const_spec((1, d)),
            _const_spec((d, 2 * hid)),
            _const_spec((FFN_CONV_WIDTH, 2 * hid)),
            _const_spec((1, 2 * hid)),
            _const_spec((hid, d)),
            _const_spec((1, d)),
        ],
        out_specs=pl.BlockSpec((None, tm, d), lambda i, j: (i, j, 0)),
        out_shape=jax.ShapeDtypeStruct((b, s, d), F32),
        scratch_shapes=[
            pltpu.VMEM((tm + FFN_HALO, 2 * hid), F32),
            pltpu.VMEM((tm, hid), BF16),
        ],
        compiler_params=_params(2),
        name="conv_ffn",
    )(x, g.reshape(1, d), w_up, dw_w, dw_b.reshape(1, 2 * hid), w_down, final_g.reshape(1, d))


def _odd_kernel(x_ref, g_ref, win_ref, bin_ref, dww_ref, dwb_ref, lng_ref, lnb_ref,
                wout_ref, bout_ref, o_ref, ybuf, cbuf, *, tm):
    c_in = CONV_INNER
    halo = CONV_HALO

    @pl.when(pl.program_id(1) == 0)
    def _():
        ybuf[0:halo, :] = jnp.zeros((halo, c_in), F32)

    x = x_ref[...]
    h = _rms(x, g_ref[...]).astype(BF16)
    a = jnp.dot(h, win_ref[:, :c_in], preferred_element_type=F32) + bin_ref[:, :c_in]
    gate = jnp.dot(h, win_ref[:, c_in:], preferred_element_type=F32) + bin_ref[:, c_in:]
    ybuf[halo:halo + tm, :] = a * jax.nn.sigmoid(gate)

    rt = 64
    first = halo - (CONV_WIDTH - 1)
    for c in range(0, c_in, LANES):
        for r in range(0, tm, rt):
            acc = jnp.broadcast_to(dwb_ref[:, c:c + LANES], (rt, LANES))
            for k in range(CONV_WIDTH):
                acc = acc + dww_ref[k:k + 1, c:c + LANES] * ybuf[first + k + r:first + k + r + rt, c:c + LANES]
            cbuf[r:r + rt, c:c + LANES] = acc

    ybuf[0:halo, :] = ybuf[tm:tm + halo, :]

    y = cbuf[...]
    mu = jnp.mean(y, axis=-1, keepdims=True)
    yc = y - mu
    y = yc * lax.rsqrt(jnp.mean(yc * yc, axis=-1, keepdims=True) + EPS) * lng_ref[...] + lnb_ref[...]
    y = (y * jax.nn.sigmoid(y)).astype(BF16)
    o_ref[...] = x + jnp.dot(y, wout_ref[...], preferred_element_type=F32) + bout_ref[...]


def _odd_mixer(x, g, w_in, b_in, dw_w, dw_b, ln_g, ln_b, w_out, b_out):
    b, s, d = x.shape
    tm = TM_ODD
    c_in = CONV_INNER
    return pl.pallas_call(
        functools.partial(_odd_kernel, tm=tm),
        grid=(b, s // tm),
        in_specs=[
            pl.BlockSpec((None, tm, d), lambda i, j: (i, j, 0)),
            _const_spec((1, d)),
            _const_spec((d, 2 * c_in)),
            _const_spec((1, 2 * c_in)),
            _const_spec((CONV_WIDTH, c_in)),
            _const_spec((1, c_in)),
            _const_spec((1, c_in)),
            _const_spec((1, c_in)),
            _const_spec((c_in, d)),
            _const_spec((1, d)),
        ],
        out_specs=pl.BlockSpec((None, tm, d), lambda i, j: (i, j, 0)),
        out_shape=jax.ShapeDtypeStruct((b, s, d), F32),
        scratch_shapes=[
            pltpu.VMEM((tm + CONV_HALO, c_in), F32),
            pltpu.VMEM((tm, c_in), F32),
        ],
        compiler_params=_params(2),
        name="odd_mixer",
    )(x, g.reshape(1, d), w_in, b_in.reshape(1, 2 * c_in), dw_w, dw_b.reshape(1, c_in),
      ln_g.reshape(1, c_in), ln_b.reshape(1, c_in), w_out, b_out.reshape(1, d))


def _gelu_tanh(x):
    return 0.5 * x * (1.0 + jnp.tanh(math.sqrt(2.0 / math.pi) * (x + 0.044715 * (x * x * x))))


def _even_in_kernel(x_ref, g_ref, w_ref, cos_ref, sin_ref, lng_ref, lnb_ref,
                    q_ref, k_ref, v_ref, u_ref, z_ref, *, tm):
    x = x_ref[...]
    h = _rms(x, g_ref[...]).astype(BF16)
    cos = cos_ref[...]
    sin = sin_ref[...]
    lane = lax.broadcasted_iota(jnp.int32, (tm, LANES), 1)
    lower = (lane % DA_HEAD_DIM) < (DA_HEAD_DIM // 2)
    half = DA_HEAD_DIM // 2

    def rope_store(dst_ref, col0, scale):
        for j in range(0, DA_QK_WIDTH, LANES):
            t = jnp.dot(h, w_ref[:, col0 + j:col0 + j + LANES], preferred_element_type=F32)
            rot = jnp.where(lower, pltpu.roll(t, LANES - half, axis=1), pltpu.roll(t, half, axis=1))
            t = t * cos + rot * sin
            if scale != 1.0:
                t = t * scale
            dst_ref[:, j:j + LANES] = t.astype(BF16)

    rope_store(q_ref, 0, DA_HEAD_DIM ** -0.5)
    rope_store(k_ref, DA_QK_WIDTH, 1.0)
    o1 = 2 * DA_QK_WIDTH
    v_ref[...] = jnp.dot(h, w_ref[:, o1:o1 + DA_V_WIDTH], preferred_element_type=F32).astype(BF16)
    o2 = o1 + DA_V_WIDTH
    u_ref[...] = _gelu_tanh(jnp.dot(h, w_ref[:, o2:o2 + SG_WIDTH], preferred_element_type=F32))
    o3 = o2 + SG_WIDTH
    for j in range(0, SG_WIDTH, SG_GROUP_DIM):
        z = _gelu_tanh(jnp.dot(h, w_ref[:, o3 + j:o3 + j + SG_GROUP_DIM], preferred_element_type=F32))
        mu = jnp.mean(z, axis=-1, keepdims=True)
        zc = z - mu
        z = zc * lax.rsqrt(jnp.mean(zc * zc, axis=-1, keepdims=True) + EPS)
        z = z * lng_ref[:, j:j + SG_GROUP_DIM] + lnb_ref[:, j:j + SG_GROUP_DIM]
        z_ref[:, j:j + SG_GROUP_DIM] = z.astype(BF16)


def _even_in(x, g, w_in, cos, sin, ln_g, ln_b):
    b, s, d = x.shape
    tm = TM_IN
    row_spec = lambda w: pl.BlockSpec((None, tm, w), lambda i, j: (i, j, 0))
    return pl.pallas_call(
        functools.partial(_even_in_kernel, tm=tm),
        grid=(b, s // tm),
        in_specs=[
            row_spec(d),
            _const_spec((1, d)),
            _const_spec((d, AB_IN_WIDTH)),
            pl.BlockSpec((tm, LANES), lambda i, j: (j, 0)),
            pl.BlockSpec((tm, LANES), lambda i, j: (j, 0)),
            _const_spec((1, SG_WIDTH)),
            _const_spec((1, SG_WIDTH)),
        ],
        out_specs=[row_spec(DA_QK_WIDTH), row_spec(DA_QK_WIDTH), row_spec(DA_V_WIDTH),
                   row_spec(SG_WIDTH), row_spec(SG_WIDTH)],
        out_shape=[
            jax.ShapeDtypeStruct((b, s, DA_QK_WIDTH), BF16),
            jax.ShapeDtypeStruct((b, s, DA_QK_WIDTH), BF16),
            jax.ShapeDtypeStruct((b, s, DA_V_WIDTH), BF16),
            jax.ShapeDtypeStruct((b, s, SG_WIDTH), F32),
            jax.ShapeDtypeStruct((b, s, SG_WIDTH), BF16),
        ],
        compiler_params=_params(2),
        name="even_in",
    )(x, g.reshape(1, d), w_in, cos, sin, ln_g.reshape(1, SG_WIDTH), ln_b.reshape(1, SG_WIDTH))


def _attn_kernel(lam_ref, q_ref, k_ref, v_ref, sg_ref, o_ref, *, tq, lambda_init):
    qi = pl.program_id(2)
    lp = lam_ref[...]
    lam = (jnp.exp(jnp.sum(lp[0:1] * lp[1:2], axis=-1, keepdims=True))
           - jnp.exp(jnp.sum(lp[2:3] * lp[3:4], axis=-1, keepdims=True)) + lambda_init)

    q = q_ref[...]
    lane = lax.broadcasted_iota(jnp.int32, (tq, LANES), 1)
    zero = jnp.zeros_like(q)
    qs = (jnp.where(lane < DA_HEAD_DIM, q, zero), jnp.where(lane >= DA_HEAD_DIM, q, zero))
    nt = (((1,), (1,)), ((), ()))

    def step(j, carry, masked):
        k = k_ref[pl.ds(pl.multiple_of(j * tq, tq), tq), :]
        v = v_ref[pl.ds(pl.multiple_of(j * tq, tq), tq), :]
        out = []
        for c in range(2):
            m, l, acc = carry[c]
            s = lax.dot_general(qs[c], k, nt, preferred_element_type=F32)
            if masked:
                row = lax.broadcasted_iota(jnp.int32, (tq, tq), 0)
                col = lax.broadcasted_iota(jnp.int32, (tq, tq), 1)
                s = jnp.where(col <= row, s, -jnp.inf)
            m_new = jnp.maximum(m, jnp.max(s, axis=-1, keepdims=True))
            alpha = jnp.exp(m - m_new)
            p = jnp.exp(s - m_new)
            l = alpha * l + jnp.sum(p, axis=-1, keepdims=True)
            acc = alpha * acc + jnp.dot(p.astype(BF16), v, preferred_element_type=F32)
            out.append((m_new, l, acc))
        return tuple(out)

    init = tuple((jnp.full((tq, 1), -jnp.inf, F32), jnp.zeros((tq, 1), F32),
                  jnp.zeros((tq, LANES), F32)) for _ in range(2))
    carry = lax.fori_loop(0, qi, lambda j, c: step(j, c, False), init)
    (_, l1, acc1), (_, l2, acc2) = step(qi, carry, True)
    o = acc1 / l1 - lam * (acc2 / l2)
    o = _rms(o, sg_ref[...]) * (1.0 - lambda_init)
    o_ref[...] = o.astype(BF16)


def _diff_attention(lam_params, q, k, v, subln_g, lambda_init):
    b, s, _ = q.shape
    tq = TQ
    return pl.pallas_call(
        functools.partial(_attn_kernel, tq=tq, lambda_init=lambda_init),
        grid=(b, DA_HEADS, s // tq),
        in_specs=[
            _const_spec((4, DA_HEAD_DIM)),
            pl.BlockSpec((None, tq, LANES), lambda i, h, j: (i, j, h)),
            pl.BlockSpec((None, s, LANES), lambda i, h, j: (i, 0, h)),
            pl.BlockSpec((None, s, LANES), lambda i, h, j: (i, 0, h)),
            _const_spec((1, DA_V_DIM)),
        ],
        out_specs=pl.BlockSpec((None, tq, LANES), lambda i, h, j: (i, j, h)),
        out_shape=jax.ShapeDtypeStruct((b, s, DA_V_WIDTH), BF16),
        compiler_params=_params(3),
        name="diff_attn",
    )(lam_params, q, k, v, subln_g.reshape(1, DA_V_DIM))


def _even_out_kernel(o_ref, u_ref, z_ref, x_ref, sgw_ref, sgb_ref, wout_ref, out_ref, cat, *, tm):
    row = lax.broadcasted_iota(jnp.int32, (SG_CHUNK, SG_CHUNK), 0)
    col = lax.broadcasted_iota(jnp.int32, (SG_CHUNK, SG_CHUNK), 1)
    tri = col <= row
    cat[:, 0:DA_V_WIDTH] = o_ref[...]
    for g in range(SG_GROUPS):
        w = jnp.where(tri, sgw_ref[g], 0.0).astype(BF16)
        bias = sgb_ref[g]
        c0 = g * SG_GROUP_DIM
        for r in range(0, tm, SG_CHUNK):
            zs = jnp.dot(w, z_ref[r:r + SG_CHUNK, c0:c0 + SG_GROUP_DIM],
                         preferred_element_type=F32) + bias
            gate = u_ref[r:r + SG_CHUNK, c0:c0 + SG_GROUP_DIM] * zs
            cat[r:r + SG_CHUNK, DA_V_WIDTH + c0:DA_V_WIDTH + c0 + SG_GROUP_DIM] = gate.astype(BF16)
    out_ref[...] = x_ref[...] + jnp.dot(cat[...], wout_ref[...], preferred_element_type=F32)


def _even_out(o, u, z, x, sg_w, sg_b, w_out):
    b, s, d = x.shape
    tm = TM_OUT
    row_spec = lambda w: pl.BlockSpec((None, tm, w), lambda i, j: (i, j, 0))
    return pl.pallas_call(
        functools.partial(_even_out_kernel, tm=tm),
        grid=(b, s // tm),
        in_specs=[
            row_spec(DA_V_WIDTH), row_spec(SG_WIDTH), row_spec(SG_WIDTH), row_spec(d),
            _const_spec((SG_GROUPS, SG_CHUNK, SG_CHUNK)),
            _const_spec((SG_GROUPS, SG_CHUNK, 1)),
            _const_spec((AB_OUT_WIDTH, d)),
        ],
        out_specs=row_spec(d),
        out_shape=jax.ShapeDtypeStruct((b, s, d), F32),
        scratch_shapes=[pltpu.VMEM((tm, AB_OUT_WIDTH), BF16)],
        compiler_params=_params(2),
        name="even_out",
    )(o, u, z, x, sg_w, sg_b.reshape(SG_GROUPS, SG_CHUNK, 1), w_out)


def _rope_tables(seq):
    dim = DA_HEAD_DIM
    inv = 1.0 / (ROPE_THETA ** (jnp.arange(0, dim, 2, dtype=F32) / dim))
    ang = jnp.arange(seq, dtype=F32)[:, None] * inv[None, :]
    ang = jnp.concatenate([ang, ang], axis=-1)
    cos = jnp.cos(ang)
    sin = jnp.sin(ang)
    sign = jnp.where(jnp.arange(dim) < dim // 2, -1.0, 1.0).astype(F32)
    reps = LANES // dim
    return jnp.tile(cos, (1, reps)), jnp.tile(sin * sign[None, :], (1, reps))


def kernel(x, norm_mix_g, norm_ffn_g, ab_w_in, ab_w_out, diff_lq1, diff_lk1, diff_lq2, diff_lk2,
           diff_subln_g, sg_ln_g, sg_ln_b, sg_w, sg_b, conv_w_in, conv_b_in, conv_dw_w, conv_dw_b,
           conv_ln_g, conv_ln_b, conv_w_out, conv_b_out, ffn_w_up, ffn_dw_w, ffn_dw_b, ffn_w_down,
           final_norm_g):
    seq = x.shape[1]
    cos, sin = _rope_tables(seq)
    ab_w_in = ab_w_in.astype(BF16)
    ab_w_out = ab_w_out.astype(BF16)
    conv_w_in = conv_w_in.astype(BF16)
    conv_w_out = conv_w_out.astype(BF16)
    ffn_w_up = ffn_w_up.astype(BF16)
    ffn_w_down = ffn_w_down.astype(BF16)
    for l in range(DEPTH):
        i = l // 2
        if l % 2 == 0:
            lambda_init = 0.8 - 0.6 * math.exp(-0.3 * l)
            q, k, v, u, z = _even_in(x, norm_mix_g[l], ab_w_in[i], cos, sin, sg_ln_g[i], sg_ln_b[i])
            lam_params = jnp.stack([diff_lq1[i], diff_lk1[i], diff_lq2[i], diff_lk2[i]])
            o = _diff_attention(lam_params, q, k, v, diff_subln_g[i], lambda_init)
            x = _even_out(o, u, z, x, sg_w[i], sg_b[i], ab_w_out[i])
        else:
            x = _odd_mixer(x, norm_mix_g[l], conv_w_in[i], conv_b_in[i], conv_dw_w[i], conv_dw_b[i],
                           conv_ln_g[i], conv_ln_b[i], conv_w_out[i], conv_b_out[i])
        x = _conv_ffn(x, norm_ffn_g[l], ffn_w_up[l], ffn_dw_w[l], ffn_dw_b[l], ffn_w_down[l],
                      final_norm_g, final=(l == DEPTH - 1))
    return x
```

```python
import functools
import math

import jax
import jax.numpy as jnp
from jax import lax
from jax.experimental import pallas as pl
from jax.experimental.pallas import tpu as pltpu

D_MODEL = 1024
DEPTH = 4
EPS = 1e-6

DA_HEADS = 4
DA_HEAD_DIM = 64
DA_V_DIM = 2 * DA_HEAD_DIM
DA_QK_WIDTH = DA_HEADS * 2 * DA_HEAD_DIM
DA_V_WIDTH = DA_HEADS * DA_V_DIM
ROPE_THETA = 10000.0

SG_GROUPS = 4
SG_CHUNK = 128
SG_GROUP_DIM = 128
SG_WIDTH = SG_GROUPS * SG_GROUP_DIM

AB_IN_WIDTH = 2 * DA_QK_WIDTH + DA_V_WIDTH + 2 * SG_WIDTH
AB_OUT_WIDTH = DA_V_WIDTH + SG_WIDTH

CONV_INNER = D_MODEL
CONV_WIDTH = 31
FFN_HIDDEN = 2816
FFN_CONV_WIDTH = 3

LANES = 128
SUBLANES = 8
VMEM_LIMIT = 56 * 1024 * 1024

F32 = jnp.float32
BF16 = jnp.bfloat16

TQ = 256
TM_IN = TQ
TM_OUT = TQ
TM_ODD = 256
TM_FFN = 256
CONV_HALO = 32
FFN_HALO = SUBLANES


def _rms(x, g):
    return x * lax.rsqrt(jnp.mean(x * x, axis=-1, keepdims=True) + EPS) * g


def _params(n_axes):
    return pltpu.CompilerParams(
        dimension_semantics=("arbitrary",) * n_axes, vmem_limit_bytes=VMEM_LIMIT)


def _const_spec(shape):
    nd = len(shape)
    return pl.BlockSpec(shape, lambda *_: (0,) * nd)


def _ffn_kernel(x_ref, g_ref, wup_ref, dww_ref, dwb_ref, wdown_ref, fg_ref, o_ref,
                hbuf, act, *, tm, final):
    hid = FFN_HIDDEN
    halo = FFN_HALO
    n_planes = 2 * hid // LANES
    gate0 = hid // LANES

    @pl.when(pl.program_id(1) == 0)
    def _():
        hbuf[:, 0:halo, :] = jnp.zeros((n_planes, halo, LANES), F32)

    x = x_ref[...]
    h = _rms(x, g_ref[...]).astype(BF16)
    nch = 512
    for c in range(0, 2 * hid, nch):
        up = jnp.dot(h, wup_ref[:, c:c + nch], preferred_element_type=F32)
        for i in range(nch // LANES):
            hbuf[c // LANES + i, halo:halo + tm, :] = up[:, i * LANES:(i + 1) * LANES]

    def conv(p):
        w = dww_ref[:, p * LANES:(p + 1) * LANES]
        y = dwb_ref[:, p * LANES:(p + 1) * LANES] + w[0:1] * hbuf[p, halo - 2:halo - 2 + tm, :]
        y = y + w[1:2] * hbuf[p, halo - 1:halo - 1 + tm, :]
        return y + w[2:3] * hbuf[p, halo:halo + tm, :]

    for p in range(gate0):
        a = conv(p)
        g = conv(gate0 + p)
        act[:, p * LANES:(p + 1) * LANES] = (a * (g * jax.nn.sigmoid(g))).astype(BF16)

    hbuf[:, 0:halo, :] = hbuf[:, tm:tm + halo, :]
    y = x + jnp.dot(act[...], wdown_ref[...], preferred_element_type=F32)
    if final:
        y = _rms(y, fg_ref[...])
    o_ref[...] = y


def _conv_ffn(x, g, w_up, dw_w, dw_b, w_down, final_g, final):
    b, s, d = x.shape
    tm = TM_FFN
    hid = FFN_HIDDEN
    return pl.pallas_call(
        functools.partial(_ffn_kernel, tm=tm, final=final),
        grid=(b, s // tm),
        in_specs=[
            pl.BlockSpec((None, tm, d), lambda i, j: (i, j, 0)),
            _const_spec((1, d)),
            _const_spec((d, 2 * hid)),
            _const_spec((FFN_CONV_WIDTH, 2 * hid)),
            _const_spec((1, 2 * hid)),
            _const_spec((hid, d)),
            _const_spec((1, d)),
        ],
        out_specs=pl.BlockSpec((None, tm, d), lambda i, j: (i, j, 0)),
        out_shape=jax.ShapeDtypeStruct((b, s, d), F32),
        scratch_shapes=[
            pltpu.VMEM((2 * hid // LANES, tm + FFN_HALO, LANES), F32),
            pltpu.VMEM((tm, hid), BF16),
        ],
        compiler_params=_params(2),
        name="conv_ffn",
    )(x, g.reshape(1, d), w_up, dw_w, dw_b.reshape(1, 2 * hid), w_down, final_g.reshape(1, d))


def _odd_kernel(x_ref, g_ref, win_ref, bin_ref, dww_ref, dwb_ref, lng_ref, lnb_ref,
                wout_ref, bout_ref, o_ref, ybuf, cbuf, *, tm):
    c_in = CONV_INNER
    halo = CONV_HALO
    n_planes = c_in // LANES

    @pl.when(pl.program_id(1) == 0)
    def _():
        ybuf[:, 0:halo, :] = jnp.zeros((n_planes, halo, LANES), F32)

    x = x_ref[...]
    h = _rms(x, g_ref[...]).astype(BF16)
    nch = 256
    for c in range(0, c_in, nch):
        a = jnp.dot(h, win_ref[:, c:c + nch], preferred_element_type=F32) + bin_ref[:, c:c + nch]
        gate = (jnp.dot(h, win_ref[:, c_in + c:c_in + c + nch], preferred_element_type=F32)
                + bin_ref[:, c_in + c:c_in + c + nch])
        y = a * jax.nn.sigmoid(gate)
        for i in range(nch // LANES):
            ybuf[c // LANES + i, halo:halo + tm, :] = y[:, i * LANES:(i + 1) * LANES]

    rt = 64
    first = halo - (CONV_WIDTH - 1)
    for p in range(n_planes):
        for r in range(0, tm, rt):
            acc = jnp.broadcast_to(dwb_ref[:, p * LANES:(p + 1) * LANES], (rt, LANES))
            for k in range(CONV_WIDTH):
                acc = acc + dww_ref[k:k + 1, p * LANES:(p + 1) * LANES] * ybuf[p, first + k + r:first + k + r + rt, :]
            cbuf[r:r + rt, p * LANES:(p + 1) * LANES] = acc

    ybuf[:, 0:halo, :] = ybuf[:, tm:tm + halo, :]

    y = cbuf[...]
    mu = jnp.mean(y, axis=-1, keepdims=True)
    yc = y - mu
    y = yc * lax.rsqrt(jnp.mean(yc * yc, axis=-1, keepdims=True) + EPS) * lng_ref[...] + lnb_ref[...]
    y = (y * jax.nn.sigmoid(y)).astype(BF16)
    o_ref[...] = x + jnp.dot(y, wout_ref[...], preferred_element_type=F32) + bout_ref[...]


def _odd_mixer(x, g, w_in, b_in, dw_w, dw_b, ln_g, ln_b, w_out, b_out):
    b, s, d = x.shape
    tm = TM_ODD
    c_in = CONV_INNER
    return pl.pallas_call(
        functools.partial(_odd_kernel, tm=tm),
        grid=(b, s // tm),
        in_specs=[
            pl.BlockSpec((None, tm, d), lambda i, j: (i, j, 0)),
            _const_spec((1, d)),
            _const_spec((d, 2 * c_in)),
            _const_spec((1, 2 * c_in)),
            _const_spec((CONV_WIDTH, c_in)),
            _const_spec((1, c_in)),
            _const_spec((1, c_in)),
            _const_spec((1, c_in)),
            _const_spec((c_in, d)),
            _const_spec((1, d)),
        ],
        out_specs=pl.BlockSpec((None, tm, d), lambda i, j: (i, j, 0)),
        out_shape=jax.ShapeDtypeStruct((b, s, d), F32),
        scratch_shapes=[
            pltpu.VMEM((c_in // LANES, tm + CONV_HALO, LANES), F32),
            pltpu.VMEM((tm, c_in), F32),
        ],
        compiler_params=_params(2),
        name="odd_mixer",
    )(x, g.reshape(1, d), w_in, b_in.reshape(1, 2 * c_in), dw_w, dw_b.reshape(1, c_in),
      ln_g.reshape(1, c_in), ln_b.reshape(1, c_in), w_out, b_out.reshape(1, d))


def _gelu_tanh(x):
    return 0.5 * x * (1.0 + jnp.tanh(math.sqrt(2.0 / math.pi) * (x + 0.044715 * (x * x * x))))


def _even_in_kernel(x_ref, g_ref, w_ref, cos_ref, sin_ref, lng_ref, lnb_ref,
                    q_ref, k_ref, v_ref, u_ref, z_ref, *, tm):
    x = x_ref[...]
    h = _rms(x, g_ref[...]).astype(BF16)
    cos = cos_ref[...]
    sin = sin_ref[...]
    lane = lax.broadcasted_iota(jnp.int32, (tm, LANES), 1)
    lower = (lane % DA_HEAD_DIM) < (DA_HEAD_DIM // 2)
    half = DA_HEAD_DIM // 2

    def rope_store(dst_ref, col0, scale):
        for j in range(0, DA_QK_WIDTH, LANES):
            t = jnp.dot(h, w_ref[:, col0 + j:col0 + j + LANES], preferred_element_type=F32)
            rot = jnp.where(lower, pltpu.roll(t, LANES - half, axis=1), pltpu.roll(t, half, axis=1))
            t = t * cos + rot * sin
            if scale != 1.0:
                t = t * scale
            dst_ref[:, j:j + LANES] = t.astype(BF16)

    rope_store(q_ref, 0, DA_HEAD_DIM ** -0.5 * math.log2(math.e))
    rope_store(k_ref, DA_QK_WIDTH, 1.0)
    o1 = 2 * DA_QK_WIDTH
    v_ref[...] = jnp.dot(h, w_ref[:, o1:o1 + DA_V_WIDTH], preferred_element_type=F32).astype(BF16)
    o2 = o1 + DA_V_WIDTH
    u_ref[...] = _gelu_tanh(jnp.dot(h, w_ref[:, o2:o2 + SG_WIDTH], preferred_element_type=F32))
    o3 = o2 + SG_WIDTH
    for j in range(0, SG_WIDTH, SG_GROUP_DIM):
        z = _gelu_tanh(jnp.dot(h, w_ref[:, o3 + j:o3 + j + SG_GROUP_DIM], preferred_element_type=F32))
        mu = jnp.mean(z, axis=-1, keepdims=True)
        zc = z - mu
        z = zc * lax.rsqrt(jnp.mean(zc * zc, axis=-1, keepdims=True) + EPS)
        z = z * lng_ref[:, j:j + SG_GROUP_DIM] + lnb_ref[:, j:j + SG_GROUP_DIM]
        z_ref[:, j:j + SG_GROUP_DIM] = z.astype(BF16)


def _even_in(x, g, w_in, cos, sin, ln_g, ln_b):
    b, s, d = x.shape
    tm = TM_IN
    n_tiles = s // tm
    row_spec = lambda w: pl.BlockSpec((None, tm, w), lambda i, j: (i, j, 0))
    q_spec = pl.BlockSpec((None, tm, DA_QK_WIDTH), lambda i, j: (i, _pair_perm(j, n_tiles), 0))
    return pl.pallas_call(
        functools.partial(_even_in_kernel, tm=tm),
        grid=(b, n_tiles),
        in_specs=[
            row_spec(d),
            _const_spec((1, d)),
            _const_spec((d, AB_IN_WIDTH)),
            pl.BlockSpec((tm, LANES), lambda i, j: (j, 0)),
            pl.BlockSpec((tm, LANES), lambda i, j: (j, 0)),
            _const_spec((1, SG_WIDTH)),
            _const_spec((1, SG_WIDTH)),
        ],
        out_specs=[q_spec, row_spec(DA_QK_WIDTH), row_spec(DA_V_WIDTH),
                   row_spec(SG_WIDTH), row_spec(SG_WIDTH)],
        out_shape=[
            jax.ShapeDtypeStruct((b, s, DA_QK_WIDTH), BF16),
            jax.ShapeDtypeStruct((b, s, DA_QK_WIDTH), BF16),
            jax.ShapeDtypeStruct((b, s, DA_V_WIDTH), BF16),
            jax.ShapeDtypeStruct((b, s, SG_WIDTH), F32),
            jax.ShapeDtypeStruct((b, s, SG_WIDTH), BF16),
        ],
        compiler_params=_params(2),
        name="even_in",
    )(x, g.reshape(1, d), w_in, cos, sin, ln_g.reshape(1, SG_WIDTH), ln_b.reshape(1, SG_WIDTH))


def _pair_perm(j, n_tiles):
    return jnp.where(j < n_tiles // 2, 2 * j, 2 * (n_tiles - 1 - j) + 1)


def _attn_kernel(lam_ref, q_ref, k_ref, v_ref, sg_ref, o_ref, qs_ref, m_ref, acc_ref, bias_ref,
                 *, tq, n_tiles, lambda_init):
    pair = pl.program_id(1)
    rows = 2 * tq
    width = 2 * tq
    last = n_tiles - 1
    nt = (((1,), (1,)), ((), ()))
    lane = lax.broadcasted_iota(jnp.int32, (tq, LANES), 1)
    ones = jnp.ones((width, LANES), BF16)

    def scores(sel, h, start):
        k = k_ref[pl.ds(start, width), h * LANES:(h + 1) * LANES]
        return lax.dot_general(qs_ref[sel, h], k, nt, preferred_element_type=F32)

    def values(h, start):
        return jnp.concatenate([v_ref[pl.ds(start, width), h * LANES:(h + 1) * LANES], ones], axis=1)

    def spread(m):
        return jnp.concatenate([m] * (width // LANES), axis=1)

    for t, tile in enumerate((pair, last - pair)):
        blk = tile // 2
        start = pl.multiple_of(blk * width, width)
        r = lax.broadcasted_iota(jnp.int32, (tq, width), 0)
        c = lax.broadcasted_iota(jnp.int32, (tq, width), 1)
        bias = jnp.where(c <= r + (tile * tq - blk * width), 0.0, -jnp.inf).astype(F32)
        bias_ref[t] = bias
        for h in range(DA_HEADS):
            qh = q_ref[t * tq:(t + 1) * tq, h * LANES:(h + 1) * LANES]
            zero = jnp.zeros_like(qh)
            qs_ref[t, h, 0:tq, :] = jnp.where(lane < DA_HEAD_DIM, qh, zero)
            qs_ref[t, h, tq:rows, :] = jnp.where(lane >= DA_HEAD_DIM, qh, zero)
            s = scores(t, h, start) + jnp.concatenate([bias_ref[t]] * 2, axis=0)
            m = jnp.broadcast_to(jnp.max(s, axis=-1, keepdims=True), (rows, LANES))
            p = jnp.exp2(s - spread(m)).astype(BF16)
            acc_ref[t, h] = jnp.dot(p, values(h, start), preferred_element_type=F32)
            m_ref[t, h] = m

    full_a = pair // 2
    for t in range((n_tiles - 2) // 2):
        first = t < full_a
        sel = jnp.where(first, 0, 1)
        start = pl.multiple_of(jnp.where(first, t, t - full_a) * width, width)
        for h in range(DA_HEADS):
            s = scores(sel, h, start)
            m_old = m_ref[sel, h]
            m_new = jnp.maximum(m_old, jnp.max(s, axis=-1, keepdims=True))
            alpha = jnp.exp2(m_old - m_new)
            p = jnp.exp2(s - spread(m_new)).astype(BF16)
            pv = jnp.dot(p, values(h, start), preferred_element_type=F32)
            acc_ref[sel, h] = jnp.concatenate([alpha, alpha], axis=1) * acc_ref[sel, h] + pv
            m_ref[sel, h] = m_new

    lp = lam_ref[...]
    lam = (jnp.exp(jnp.sum(lp[0:1] * lp[1:2], axis=-1, keepdims=True))
           - jnp.exp(jnp.sum(lp[2:3] * lp[3:4], axis=-1, keepdims=True)) + lambda_init)
    for t in range(2):
        for h in range(DA_HEADS):
            o1 = acc_ref[t, h, 0:tq, 0:LANES] / acc_ref[t, h, 0:tq, LANES:2 * LANES]
            o2 = acc_ref[t, h, tq:rows, 0:LANES] / acc_ref[t, h, tq:rows, LANES:2 * LANES]
            o = _rms(o1 - lam * o2, sg_ref[...]) * (1.0 - lambda_init)
            o_ref[t * tq:(t + 1) * tq, h * LANES:(h + 1) * LANES] = o.astype(BF16)


def _diff_attention(lam_params, q, k, v, subln_g, lambda_init):
    b, s, w = q.shape
    tq = TQ
    n_tiles = s // tq
    return pl.pallas_call(
        functools.partial(_attn_kernel, tq=tq, n_tiles=n_tiles, lambda_init=lambda_init),
        grid=(b, n_tiles // 2),
        in_specs=[
            _const_spec((4, DA_HEAD_DIM)),
            pl.BlockSpec((None, 2 * tq, w), lambda i, j: (i, j, 0)),
            pl.BlockSpec((None, s, w), lambda i, j: (i, 0, 0)),
            pl.BlockSpec((None, s, w), lambda i, j: (i, 0, 0)),
            _const_spec((1, DA_V_DIM)),
        ],
        out_specs=pl.BlockSpec((None, 2 * tq, w), lambda i, j: (i, j, 0)),
        out_shape=jax.ShapeDtypeStruct((b, s, DA_V_WIDTH), BF16),
        scratch_shapes=[
            pltpu.VMEM((2, DA_HEADS, 2 * tq, LANES), BF16),
            pltpu.VMEM((2, DA_HEADS, 2 * tq, LANES), F32),
            pltpu.VMEM((2, DA_HEADS, 2 * tq, 2 * LANES), F32),
            pltpu.VMEM((2, tq, 2 * tq), F32),
        ],
        compiler_params=_params(2),
        name="diff_attn",
    )(lam_params, q, k, v, subln_g.reshape(1, DA_V_DIM))


def _even_out_kernel(o_ref, u_ref, z_ref, x_ref, sgw_ref, sgb_ref, wout_ref, out_ref, cat, *, tm):
    row = lax.broadcasted_iota(jnp.int32, (SG_CHUNK, SG_CHUNK), 0)
    col = lax.broadcasted_iota(jnp.int32, (SG_CHUNK, SG_CHUNK), 1)
    tri = col <= row
    cat[:, 0:DA_V_WIDTH] = o_ref[...]
    for g in range(SG_GROUPS):
        w = jnp.where(tri, sgw_ref[g], 0.0).astype(BF16)
        bias = sgb_ref[g]
        c0 = g * SG_GROUP_DIM
        for r in range(0, tm, SG_CHUNK):
            zs = jnp.dot(w, z_ref[r:r + SG_CHUNK, c0:c0 + SG_GROUP_DIM],
                         preferred_element_type=F32) + bias
            gate = u_ref[r:r + SG_CHUNK, c0:c0 + SG_GROUP_DIM] * zs
            cat[r:r + SG_CHUNK, DA_V_WIDTH + c0:DA_V_WIDTH + c0 + SG_GROUP_DIM] = gate.astype(BF16)
    out_ref[...] = x_ref[...] + jnp.dot(cat[...], wout_ref[...], preferred_element_type=F32)


def _even_out(o, u, z, x, sg_w, sg_b, w_out):
    b, s, d = x.shape
    tm = TM_OUT
    n_tiles = s // tm
    row_spec = lambda w: pl.BlockSpec((None, tm, w), lambda i, j: (i, j, 0))
    o_spec = pl.BlockSpec((None, tm, DA_V_WIDTH), lambda i, j: (i, _pair_perm(j, n_tiles), 0))
    return pl.pallas_call(
        functools.partial(_even_out_kernel, tm=tm),
        grid=(b, n_tiles),
        in_specs=[
            o_spec, row_spec(SG_WIDTH), row_spec(SG_WIDTH), row_spec(d),
            _const_spec((SG_GROUPS, SG_CHUNK, SG_CHUNK)),
            _const_spec((SG_GROUPS, SG_CHUNK, 1)),
            _const_spec((AB_OUT_WIDTH, d)),
        ],
        out_specs=row_spec(d),
        out_shape=jax.ShapeDtypeStruct((b, s, d), F32),
        scratch_shapes=[pltpu.VMEM((tm, AB_OUT_WIDTH), BF16)],
        compiler_params=_params(2),
        name="even_out",
    )(o, u, z, x, sg_w, sg_b.reshape(SG_GROUPS, SG_CHUNK, 1), w_out)


def _rope_tables(seq):
    dim = DA_HEAD_DIM
    inv = 1.0 / (ROPE_THETA ** (jnp.arange(0, dim, 2, dtype=F32) / dim))
    ang = jnp.arange(seq, dtype=F32)[:, None] * inv[None, :]
    ang = jnp.concatenate([ang, ang], axis=-1)
    cos = jnp.cos(ang)
    sin = jnp.sin(ang)
    sign = jnp.where(jnp.arange(dim) < dim // 2, -1.0, 1.0).astype(F32)
    reps = LANES // dim
    return jnp.tile(cos, (1, reps)), jnp.tile(sin * sign[None, :], (1, reps))


def kernel(x, norm_mix_g, norm_ffn_g, ab_w_in, ab_w_out, diff_lq1, diff_lk1, diff_lq2, diff_lk2,
           diff_subln_g, sg_ln_g, sg_ln_b, sg_w, sg_b, conv_w_in, conv_b_in, conv_dw_w, conv_dw_b,
           conv_ln_g, conv_ln_b, conv_w_out, conv_b_out, ffn_w_up, ffn_dw_w, ffn_dw_b, ffn_w_down,
           final_norm_g):
    seq = x.shape[1]
    cos, sin = _rope_tables(seq)
    ab_w_in = ab_w_in.astype(BF16)
    ab_w_out = ab_w_out.astype(BF16)
    conv_w_in = conv_w_in.astype(BF16)
    conv_w_out = conv_w_out.astype(BF16)
    ffn_w_up = ffn_w_up.astype(BF16)
    ffn_w_down = ffn_w_down.astype(BF16)
    for l in range(DEPTH):
        i = l // 2
        if l % 2 == 0:
            lambda_init = 0.8 - 0.6 * math.exp(-0.3 * l)
            q, k, v, u, z = _even_in(x, norm_mix_g[l], ab_w_in[i], cos, sin, sg_ln_g[i], sg_ln_b[i])
            lam_params = jnp.stack([diff_lq1[i], diff_lk1[i], diff_lq2[i], diff_lk2[i]])
            o = _diff_attention(lam_params, q, k, v, diff_subln_g[i], lambda_init)
            x = _even_out(o, u, z, x, sg_w[i], sg_b[i], ab_w_out[i])
        else:
            x = _odd_mixer(x, norm_mix_g[l], conv_w_in[i], conv_b_in[i], conv_dw_w[i], conv_dw_b[i],
                           conv_ln_g[i], conv_ln_b[i], conv_w_out[i], conv_b_out[i])
        x = _conv_ffn(x, norm_ffn_g[l], ffn_w_up[l], ffn_dw_w[l], ffn_dw_b[l], ffn_w_down[l],
                      final_norm_g, final=(l == DEPTH - 1))
    return x
```

```python
import functools
import math

import jax
import jax.numpy as jnp
from jax import lax
from jax.experimental import pallas as pl
from jax.experimental.pallas import tpu as pltpu

D_MODEL = 1024
DEPTH = 4
EPS = 1e-6

DA_HEADS = 4
DA_HEAD_DIM = 64
DA_V_DIM = 2 * DA_HEAD_DIM
DA_QK_WIDTH = DA_HEADS * 2 * DA_HEAD_DIM
DA_V_WIDTH = DA_HEADS * DA_V_DIM
ROPE_THETA = 10000.0

SG_GROUPS = 4
SG_CHUNK = 128
SG_GROUP_DIM = 128
SG_WIDTH = SG_GROUPS * SG_GROUP_DIM

AB_IN_WIDTH = 2 * DA_QK_WIDTH + DA_V_WIDTH + 2 * SG_WIDTH
AB_OUT_WIDTH = DA_V_WIDTH + SG_WIDTH

CONV_INNER = D_MODEL
CONV_WIDTH = 31
FFN_HIDDEN = 2816
FFN_CONV_WIDTH = 3

LANES = 128
SUBLANES = 8
VMEM_LIMIT = 56 * 1024 * 1024

F32 = jnp.float32
BF16 = jnp.bfloat16

TQ = 256
TM_IN = TQ
TM_FFN = TQ
ODD_CONV_ROWS = 64
CONV_HALO = 32
FFN_HALO = SUBLANES


def _rms(x, g):
    return x * lax.rsqrt(jnp.mean(x * x, axis=-1, keepdims=True) + EPS) * g


def _params(n_axes):
    return pltpu.CompilerParams(
        dimension_semantics=("arbitrary",) * n_axes, vmem_limit_bytes=VMEM_LIMIT)


def _const_spec(shape):
    nd = len(shape)
    return pl.BlockSpec(shape, lambda *_: (0,) * nd)


def _layer_spec(arr, l, buffers=None):
    tail = arr.shape[1:]
    zeros = (0,) * len(tail)
    kw = {} if buffers is None else {"pipeline_mode": pl.Buffered(buffers)}
    return pl.BlockSpec((None,) + tail, lambda *_: (l,) + zeros, **kw)


def _rows(a):
    return a.reshape(a.shape[0], 1, a.shape[1])


def _ffn_body(x, g, wup_ref, dww_ref, dwb_ref, wdown_ref, hbuf, act, tm):
    hid = FFN_HIDDEN
    halo = FFN_HALO
    gate0 = hid // LANES
    nch = 2 * LANES
    h = _rms(x, g).astype(BF16)

    def conv(p):
        w = dww_ref[:, p * LANES:(p + 1) * LANES]
        y = dwb_ref[:, p * LANES:(p + 1) * LANES] + w[0:1] * hbuf[p, halo - 2:halo - 2 + tm, :]
        y = y + w[1:2] * hbuf[p, halo - 1:halo - 1 + tm, :]
        return y + w[2:3] * hbuf[p, halo:halo + tm, :]

    for c in range(0, hid, nch):
        for c0 in (c, hid + c):
            up = jnp.dot(h, wup_ref[:, c0:c0 + nch], preferred_element_type=F32)
            for i in range(nch // LANES):
                hbuf[c0 // LANES + i, halo:halo + tm, :] = up[:, i * LANES:(i + 1) * LANES]
        for p in range(c // LANES, (c + nch) // LANES):
            a = conv(p)
            gt = conv(gate0 + p)
            act[:, p * LANES:(p + 1) * LANES] = (a * (gt * jax.nn.sigmoid(gt))).astype(BF16)

    hbuf[:, 0:halo, :] = hbuf[:, tm:tm + halo, :]
    return x + jnp.dot(act[...], wdown_ref[...], preferred_element_type=F32)


def _odd_front(x, g, win_ref, bin_ref, ybuf, tm):
    c_in = CONV_INNER
    halo = CONV_HALO
    h = _rms(x, g).astype(BF16)
    nch = 256
    for c in range(0, c_in, nch):
        a = jnp.dot(h, win_ref[:, c:c + nch], preferred_element_type=F32) + bin_ref[:, c:c + nch]
        gate = (jnp.dot(h, win_ref[:, c_in + c:c_in + c + nch], preferred_element_type=F32)
                + bin_ref[:, c_in + c:c_in + c + nch])
        y = a * jax.nn.sigmoid(gate)
        for i in range(nch // LANES):
            ybuf[c // LANES + i, halo:halo + tm, :] = y[:, i * LANES:(i + 1) * LANES]


def _odd_back(x, dww_ref, dwb_ref, lng_ref, lnb_ref, wout_ref, bout_ref, ybuf, cbuf, tm):
    c_in = CONV_INNER
    halo = CONV_HALO
    rt = ODD_CONV_ROWS
    first = halo - (CONV_WIDTH - 1)
    for p in range(c_in // LANES):
        for r in range(0, tm, rt):
            acc = jnp.broadcast_to(dwb_ref[:, p * LANES:(p + 1) * LANES], (rt, LANES))
            for k in range(CONV_WIDTH):
                acc = acc + dww_ref[k:k + 1, p * LANES:(p + 1) * LANES] * ybuf[p, first + k + r:first + k + r + rt, :]
            cbuf[r:r + rt, p * LANES:(p + 1) * LANES] = acc

    ybuf[:, 0:halo, :] = ybuf[:, tm:tm + halo, :]

    y = cbuf[...]
    mu = jnp.mean(y, axis=-1, keepdims=True)
    yc = y - mu
    y = yc * lax.rsqrt(jnp.mean(yc * yc, axis=-1, keepdims=True) + EPS) * lng_ref[...] + lnb_ref[...]
    y = (y * jax.nn.sigmoid(y)).astype(BF16)
    return x + jnp.dot(y, wout_ref[...], preferred_element_type=F32) + bout_ref[...]


def _even_out_body(o_ref, u_ref, z_ref, x_ref, sgw_ref, sgb_ref, wout_ref, cat, tm):
    row = lax.broadcasted_iota(jnp.int32, (SG_CHUNK, SG_CHUNK), 0)
    col = lax.broadcasted_iota(jnp.int32, (SG_CHUNK, SG_CHUNK), 1)
    tri = col <= row
    cat[:, 0:DA_V_WIDTH] = o_ref[...]
    for g in range(SG_GROUPS):
        w = jnp.where(tri, sgw_ref[g], 0.0).astype(BF16)
        bias = sgb_ref[g]
        c0 = g * SG_GROUP_DIM
        for r in range(0, tm, SG_CHUNK):
            zs = jnp.dot(w, z_ref[r:r + SG_CHUNK, c0:c0 + SG_GROUP_DIM],
                         preferred_element_type=F32) + bias
            gate = u_ref[r:r + SG_CHUNK, c0:c0 + SG_GROUP_DIM] * zs
            cat[r:r + SG_CHUNK, DA_V_WIDTH + c0:DA_V_WIDTH + c0 + SG_GROUP_DIM] = gate.astype(BF16)
    return x_ref[...] + jnp.dot(cat[...], wout_ref[...], preferred_element_type=F32)


N_MIXER_REFS = {"odd": 10, "even": 7}


def _tail_kernel(*refs, mode, tm, n_tiles, n_steps, final):
    n_mix = N_MIXER_REFS[mode]
    mix_refs = refs[:n_mix]
    fg_ref, wup_ref, fdw_ref, fdb_ref, wdown_ref, fing_ref, out_ref = refs[n_mix:n_mix + 7]
    x1buf, hbuf, act = refs[n_mix + 7:n_mix + 10]
    mix_scratch = refs[n_mix + 10:]
    g = pl.program_id(0)
    tile_a = jnp.minimum(g, n_steps - 1) % n_tiles
    tile_b = jnp.maximum(g - 1, 0) % n_tiles

    @pl.when(g == 0)
    def _():
        x1buf[...] = jnp.zeros(x1buf.shape, F32)

    @pl.when(tile_b == 0)
    def _():
        hbuf[:, 0:FFN_HALO, :] = jnp.zeros((hbuf.shape[0], FFN_HALO, LANES), F32)

    if mode == "odd":
        ybuf = mix_scratch[0]

        @pl.when(tile_a == 0)
        def _():
            ybuf[:, 0:CONV_HALO, :] = jnp.zeros((ybuf.shape[0], CONV_HALO, LANES), F32)

    if mode == "odd":
        x_ref, mg_ref, win_ref, bin_ref = mix_refs[:4]
        ybuf, cbuf = mix_scratch
        _odd_front(x_ref[...], mg_ref[...], win_ref, bin_ref, ybuf, tm)

    y = _ffn_body(x1buf[(g + 1) % 2], fg_ref[...], wup_ref, fdw_ref, fdb_ref, wdown_ref, hbuf, act, tm)
    if final:
        y = _rms(y, fing_ref[...])
    out_ref[...] = y

    if mode == "odd":
        x1 = _odd_back(x_ref[...], *mix_refs[4:], ybuf, cbuf, tm)
    else:
        x1 = _even_out_body(*mix_refs, *mix_scratch, tm)
    x1buf[g % 2] = x1


def _layer_tail(mode, l, mix_args, mix_specs, mix_scratch, shape, norm_ffn_g, ffn_w_up, ffn_dw_w,
                ffn_dw_b, ffn_w_down, final_norm_g):
    b, s, d = shape
    tm = TM_FFN
    hid = FFN_HIDDEN
    n_tiles = s // tm
    n_steps = b * n_tiles
    tile_b = lambda g: jnp.maximum(g - 1, 0)
    return pl.pallas_call(
        functools.partial(_tail_kernel, mode=mode, tm=tm, n_tiles=n_tiles, n_steps=n_steps,
                          final=(l == DEPTH - 1)),
        grid=(n_steps + 1,),
        in_specs=mix_specs + [
            _layer_spec(_rows(norm_ffn_g), l),
            _layer_spec(ffn_w_up, l, buffers=1),
            _layer_spec(ffn_dw_w, l),
            _layer_spec(_rows(ffn_dw_b), l),
            _layer_spec(ffn_w_down, l, buffers=1),
            _const_spec((1, d)),
        ],
        out_specs=pl.BlockSpec((None, tm, d), lambda g: (tile_b(g) // n_tiles, tile_b(g) % n_tiles, 0)),
        out_shape=jax.ShapeDtypeStruct((b, s, d), F32),
        scratch_shapes=[
            pltpu.VMEM((2, tm, d), F32),
            pltpu.VMEM((2 * hid // LANES, tm + FFN_HALO, LANES), F32),
            pltpu.VMEM((tm, hid), BF16),
        ] + mix_scratch,
        compiler_params=_params(1),
        name=mode + "_tail",
    )(*mix_args, _rows(norm_ffn_g), ffn_w_up, ffn_dw_w, _rows(ffn_dw_b), ffn_w_down,
      final_norm_g.reshape(1, d))


def _tile_a_spec(width, n_tiles, n_steps, perm=None):
    def index(g):
        t = jnp.minimum(g, n_steps - 1)
        j = t % n_tiles
        return (t // n_tiles, j if perm is None else perm(j, n_tiles), 0)
    return pl.BlockSpec((None, TM_FFN, width), index)


def _odd_layer(x, l, norm_mix_g, conv_w_in, conv_b_in, conv_dw_w, conv_dw_b, conv_ln_g, conv_ln_b,
               conv_w_out, conv_b_out, *ffn_args):
    b, s, d = x.shape
    i = l // 2
    c_in = CONV_INNER
    n_tiles = s // TM_FFN
    specs = [
        _tile_a_spec(d, n_tiles, b * n_tiles),
        _layer_spec(_rows(norm_mix_g), l),
        _layer_spec(conv_w_in, i, buffers=1),
        _layer_spec(_rows(conv_b_in), i),
        _layer_spec(conv_dw_w, i),
        _layer_spec(_rows(conv_dw_b), i),
        _layer_spec(_rows(conv_ln_g), i),
        _layer_spec(_rows(conv_ln_b), i),
        _layer_spec(conv_w_out, i, buffers=1),
        _layer_spec(_rows(conv_b_out), i),
    ]
    args = [x, _rows(norm_mix_g), conv_w_in, _rows(conv_b_in), conv_dw_w, _rows(conv_dw_b),
            _rows(conv_ln_g), _rows(conv_ln_b), conv_w_out, _rows(conv_b_out)]
    scratch = [
        pltpu.VMEM((c_in // LANES, TM_FFN + CONV_HALO, LANES), F32),
        pltpu.VMEM((TM_FFN, c_in), F32),
    ]
    return _layer_tail("odd", l, args, specs, scratch, x.shape, *ffn_args)


def _even_tail(o, u, z, x, l, sg_w, sg_b, ab_w_out, *ffn_args):
    b, s, d = x.shape
    i = l // 2
    n_tiles = s // TM_FFN
    n_steps = b * n_tiles
    sg_b = sg_b.reshape(sg_b.shape + (1,))
    specs = [
        _tile_a_spec(DA_V_WIDTH, n_tiles, n_steps, perm=_pair_perm),
        _tile_a_spec(SG_WIDTH, n_tiles, n_steps),
        _tile_a_spec(SG_WIDTH, n_tiles, n_steps),
        _tile_a_spec(d, n_tiles, n_steps),
        _layer_spec(sg_w, i),
        _layer_spec(sg_b, i),
        _layer_spec(ab_w_out, i, buffers=1),
    ]
    scratch = [pltpu.VMEM((TM_FFN, AB_OUT_WIDTH), BF16)]
    return _layer_tail("even", l, [o, u, z, x, sg_w, sg_b, ab_w_out], specs, scratch, x.shape, *ffn_args)


def _gelu_tanh(x):
    return 0.5 * x * (1.0 + jnp.tanh(math.sqrt(2.0 / math.pi) * (x + 0.044715 * (x * x * x))))


def _even_in_kernel(x_ref, g_ref, w_ref, cos_ref, sin_ref, lng_ref, lnb_ref,
                    q_ref, k_ref, v_ref, u_ref, z_ref, *, tm):
    x = x_ref[...]
    h = _rms(x, g_ref[...]).astype(BF16)
    cos = cos_ref[...]
    sin = sin_ref[...]
    lane = lax.broadcasted_iota(jnp.int32, (tm, LANES), 1)
    lower = (lane % DA_HEAD_DIM) < (DA_HEAD_DIM // 2)
    half = DA_HEAD_DIM // 2

    def rope_store(dst_ref, col0, scale):
        t_all = jnp.dot(h, w_ref[:, col0:col0 + DA_QK_WIDTH], preferred_element_type=F32)
        cos_s = cos * scale
        sin_s = sin * scale
        for j in range(0, DA_QK_WIDTH, LANES):
            t = t_all[:, j:j + LANES]
            rot = jnp.where(lower, pltpu.roll(t, LANES - half, axis=1), pltpu.roll(t, half, axis=1))
            dst_ref[:, j:j + LANES] = (t * cos_s + rot * sin_s).astype(BF16)

    rope_store(q_ref, 0, DA_HEAD_DIM ** -0.5 * math.log2(math.e))
    rope_store(k_ref, DA_QK_WIDTH, 1.0)
    o1 = 2 * DA_QK_WIDTH
    v_ref[...] = jnp.dot(h, w_ref[:, o1:o1 + DA_V_WIDTH], preferred_element_type=F32).astype(BF16)
    o2 = o1 + DA_V_WIDTH
    u_ref[...] = _gelu_tanh(jnp.dot(h, w_ref[:, o2:o2 + SG_WIDTH], preferred_element_type=F32))
    o3 = o2 + SG_WIDTH
    z_all = _gelu_tanh(jnp.dot(h, w_ref[:, o3:o3 + SG_WIDTH], preferred_element_type=F32))
    for j in range(0, SG_WIDTH, SG_GROUP_DIM):
        z = z_all[:, j:j + SG_GROUP_DIM]
        mu = jnp.mean(z, axis=-1, keepdims=True)
        zc = z - mu
        z = zc * lax.rsqrt(jnp.mean(zc * zc, axis=-1, keepdims=True) + EPS)
        z = z * lng_ref[:, j:j + SG_GROUP_DIM] + lnb_ref[:, j:j + SG_GROUP_DIM]
        z_ref[:, j:j + SG_GROUP_DIM] = z.astype(BF16)


def _even_in(x, l, norm_mix_g, ab_w_in, cos, sin, sg_ln_g, sg_ln_b):
    b, s, d = x.shape
    i = l // 2
    tm = TM_IN
    n_tiles = s // tm
    row_spec = lambda w: pl.BlockSpec((None, tm, w), lambda i, j: (i, j, 0))
    q_spec = pl.BlockSpec((None, tm, DA_QK_WIDTH), lambda i, j: (i, _pair_perm(j, n_tiles), 0))
    return pl.pallas_call(
        functools.partial(_even_in_kernel, tm=tm),
        grid=(b, n_tiles),
        in_specs=[
            row_spec(d),
            _layer_spec(_rows(norm_mix_g), l),
            _layer_spec(ab_w_in, i),
            pl.BlockSpec((tm, LANES), lambda i, j: (j, 0)),
            pl.BlockSpec((tm, LANES), lambda i, j: (j, 0)),
            _layer_spec(_rows(sg_ln_g), i),
            _layer_spec(_rows(sg_ln_b), i),
        ],
        out_specs=[q_spec, row_spec(DA_QK_WIDTH), row_spec(DA_V_WIDTH),
                   row_spec(SG_WIDTH), row_spec(SG_WIDTH)],
        out_shape=[
            jax.ShapeDtypeStruct((b, s, DA_QK_WIDTH), BF16),
            jax.ShapeDtypeStruct((b, s, DA_QK_WIDTH), BF16),
            jax.ShapeDtypeStruct((b, s, DA_V_WIDTH), BF16),
            jax.ShapeDtypeStruct((b, s, SG_WIDTH), F32),
            jax.ShapeDtypeStruct((b, s, SG_WIDTH), BF16),
        ],
        compiler_params=_params(2),
        name="even_in",
    )(x, _rows(norm_mix_g), ab_w_in, cos, sin, _rows(sg_ln_g), _rows(sg_ln_b))


def _pair_perm(j, n_tiles):
    return jnp.where(j < n_tiles // 2, 2 * j, 2 * (n_tiles - 1 - j) + 1)


def _attn_kernel(lam_ref, q_ref, k_ref, v_ref, sg_ref, o_ref, qs_ref, m_ref, acc_ref, bias_ref,
                 *, tq, n_tiles, lambda_init):
    pair = pl.program_id(1)
    rows = 2 * tq
    width = 2 * tq
    last = n_tiles - 1
    nt = (((1,), (1,)), ((), ()))
    lane = lax.broadcasted_iota(jnp.int32, (tq, LANES), 1)
    ones = jnp.ones((width, LANES), BF16)

    def scores(sel, h, start):
        k = k_ref[pl.ds(start, width), h * LANES:(h + 1) * LANES]
        return lax.dot_general(qs_ref[sel, h], k, nt, preferred_element_type=F32)

    def values(h, start):
        return jnp.concatenate([v_ref[pl.ds(start, width), h * LANES:(h + 1) * LANES], ones], axis=1)

    def spread(m):
        return jnp.concatenate([m] * (width // LANES), axis=1)

    for t, tile in enumerate((pair, last - pair)):
        blk = tile // 2
        start = pl.multiple_of(blk * width, width)
        r = lax.broadcasted_iota(jnp.int32, (tq, width), 0)
        c = lax.broadcasted_iota(jnp.int32, (tq, width), 1)
        bias = jnp.where(c <= r + (tile * tq - blk * width), 0.0, -jnp.inf).astype(F32)
        bias_ref[t] = bias
        for h in range(DA_HEADS):
            qh = q_ref[t * tq:(t + 1) * tq, h * LANES:(h + 1) * LANES]
            zero = jnp.zeros_like(qh)
            qs_ref[t, h, 0:tq, :] = jnp.where(lane < DA_HEAD_DIM, qh, zero)
            qs_ref[t, h, tq:rows, :] = jnp.where(lane >= DA_HEAD_DIM, qh, zero)
            s = scores(t, h, start) + jnp.concatenate([bias_ref[t]] * 2, axis=0)
            m = jnp.broadcast_to(jnp.max(s, axis=-1, keepdims=True), (rows, LANES))
            p = jnp.exp2(s - spread(m)).astype(BF16)
            acc_ref[t, h] = jnp.dot(p, values(h, start), preferred_element_type=F32)
            m_ref[t, h] = m

    full_a = pair // 2
    for t in range((n_tiles - 2) // 2):
        first = t < full_a
        sel = jnp.where(first, 0, 1)
        start = pl.multiple_of(jnp.where(first, t, t - full_a) * width, width)
        for h in range(DA_HEADS):
            s = scores(sel, h, start)
            m_old = m_ref[sel, h]
            m_new = jnp.maximum(m_old, jnp.max(s, axis=-1, keepdims=True))
            alpha = jnp.exp2(m_old - m_new)
            p = jnp.exp2(s - spread(m_new)).astype(BF16)
            pv = jnp.dot(p, values(h, start), preferred_element_type=F32)
            acc_ref[sel, h] = jnp.concatenate([alpha, alpha], axis=1) * acc_ref[sel, h] + pv
            m_ref[sel, h] = m_new

    lp = lam_ref[...]
    lam = (jnp.exp(jnp.sum(lp[0:1] * lp[1:2], axis=-1, keepdims=True))
           - jnp.exp(jnp.sum(lp[2:3] * lp[3:4], axis=-1, keepdims=True)) + lambda_init)
    for t in range(2):
        for h in range(DA_HEADS):
            o1 = acc_ref[t, h, 0:tq, 0:LANES] / acc_ref[t, h, 0:tq, LANES:2 * LANES]
            o2 = acc_ref[t, h, tq:rows, 0:LANES] / acc_ref[t, h, tq:rows, LANES:2 * LANES]
            o = _rms(o1 - lam * o2, sg_ref[...]) * (1.0 - lambda_init)
            o_ref[t * tq:(t + 1) * tq, h * LANES:(h + 1) * LANES] = o.astype(BF16)


def _diff_attention(l, lam_params, q, k, v, subln_g, lambda_init):
    b, s, w = q.shape
    tq = TQ
    n_tiles = s // tq
    return pl.pallas_call(
        functools.partial(_attn_kernel, tq=tq, n_tiles=n_tiles, lambda_init=lambda_init),
        grid=(b, n_tiles // 2),
        in_specs=[
            _layer_spec(lam_params, l // 2),
            pl.BlockSpec((None, 2 * tq, w), lambda i, j: (i, j, 0)),
            pl.BlockSpec((None, s, w), lambda i, j: (i, 0, 0)),
            pl.BlockSpec((None, s, w), lambda i, j: (i, 0, 0)),
            _layer_spec(_rows(subln_g), l // 2),
        ],
        out_specs=pl.BlockSpec((None, 2 * tq, w), lambda i, j: (i, j, 0)),
        out_shape=jax.ShapeDtypeStruct((b, s, DA_V_WIDTH), BF16),
        scratch_shapes=[
            pltpu.VMEM((2, DA_HEADS, 2 * tq, LANES), BF16),
            pltpu.VMEM((2, DA_HEADS, 2 * tq, LANES), F32),
            pltpu.VMEM((2, DA_HEADS, 2 * tq, 2 * LANES), F32),
            pltpu.VMEM((2, tq, 2 * tq), F32),
        ],
        compiler_params=_params(2),
        name="diff_attn",
    )(lam_params, q, k, v, _rows(subln_g))


def _rope_tables(seq):
    dim = DA_HEAD_DIM
    inv = 1.0 / (ROPE_THETA ** (jnp.arange(0, dim, 2, dtype=F32) / dim))
    ang = jnp.arange(seq, dtype=F32)[:, None] * inv[None, :]
    ang = jnp.concatenate([ang, ang], axis=-1)
    cos = jnp.cos(ang)
    sin = jnp.sin(ang)
    sign = jnp.where(jnp.arange(dim) < dim // 2, -1.0, 1.0).astype(F32)
    reps = LANES // dim
    return jnp.tile(cos, (1, reps)), jnp.tile(sin * sign[None, :], (1, reps))


def kernel(x, norm_mix_g, norm_ffn_g, ab_w_in, ab_w_out, diff_lq1, diff_lk1, diff_lq2, diff_lk2,
           diff_subln_g, sg_ln_g, sg_ln_b, sg_w, sg_b, conv_w_in, conv_b_in, conv_dw_w, conv_dw_b,
           conv_ln_g, conv_ln_b, conv_w_out, conv_b_out, ffn_w_up, ffn_dw_w, ffn_dw_b, ffn_w_down,
           final_norm_g):
    seq = x.shape[1]
    cos, sin = _rope_tables(seq)
    ab_w_in = ab_w_in.astype(BF16)
    ab_w_out = ab_w_out.astype(BF16)
    conv_w_in = conv_w_in.astype(BF16)
    conv_w_out = conv_w_out.astype(BF16)
    ffn_w_up = ffn_w_up.astype(BF16)
    ffn_w_down = ffn_w_down.astype(BF16)
    lam_params = jnp.stack([diff_lq1, diff_lk1, diff_lq2, diff_lk2], axis=1)
    ffn_args = (norm_ffn_g, ffn_w_up, ffn_dw_w, ffn_dw_b, ffn_w_down, final_norm_g)
    for l in range(DEPTH):
        if l % 2 == 0:
            lambda_init = 0.8 - 0.6 * math.exp(-0.3 * l)
            q, k, v, u, z = _even_in(x, l, norm_mix_g, ab_w_in, cos, sin, sg_ln_g, sg_ln_b)
            o = _diff_attention(l, lam_params, q, k, v, diff_subln_g, lambda_init)
            x = _even_tail(o, u, z, x, l, sg_w, sg_b, ab_w_out, *ffn_args)
        else:
            x = _odd_layer(x, l, norm_mix_g, conv_w_in, conv_b_in, conv_dw_w, conv_dw_b, conv_ln_g,
                           conv_ln_b, conv_w_out, conv_b_out, *ffn_args)
    return x
```

```python
import functools
import math

import jax
import jax.numpy as jnp
from jax import lax
from jax.experimental import pallas as pl
from jax.experimental.pallas import tpu as pltpu

D_MODEL = 1024
DEPTH = 4
EPS = 1e-6

DA_HEADS = 4
DA_HEAD_DIM = 64
DA_V_DIM = 2 * DA_HEAD_DIM
DA_QK_WIDTH = DA_HEADS * 2 * DA_HEAD_DIM
DA_V_WIDTH = DA_HEADS * DA_V_DIM
ROPE_THETA = 10000.0

SG_GROUPS = 4
SG_CHUNK = 128
SG_GROUP_DIM = 128
SG_WIDTH = SG_GROUPS * SG_GROUP_DIM

AB_IN_WIDTH = 2 * DA_QK_WIDTH + DA_V_WIDTH + 2 * SG_WIDTH
AB_OUT_WIDTH = DA_V_WIDTH + SG_WIDTH

CONV_INNER = D_MODEL
CONV_WIDTH = 31
FFN_HIDDEN = 2816
FFN_CONV_WIDTH = 3

LANES = 128
SUBLANES = 8
VMEM_LIMIT = 56 * 1024 * 1024

F32 = jnp.float32
BF16 = jnp.bfloat16

TQ = 256
TM_IN = TQ
TM_FFN = TQ
ODD_CONV_ROWS = 64
CONV_HALO = 32
FFN_HALO = SUBLANES


def _rms(x, g):
    return x * lax.rsqrt(jnp.mean(x * x, axis=-1, keepdims=True) + EPS) * g


def _params(n_axes):
    return pltpu.CompilerParams(
        dimension_semantics=("arbitrary",) * n_axes, vmem_limit_bytes=VMEM_LIMIT)


def _const_spec(shape):
    nd = len(shape)
    return pl.BlockSpec(shape, lambda *_: (0,) * nd)


def _layer_spec(arr, l, buffers=None):
    tail = arr.shape[1:]
    zeros = (0,) * len(tail)
    kw = {} if buffers is None else {"pipeline_mode": pl.Buffered(buffers)}
    return pl.BlockSpec((None,) + tail, lambda *_: (l,) + zeros, **kw)


def _rows(a):
    return a.reshape(a.shape[0], 1, a.shape[1])


def _ffn_body(x, g, wup_ref, dww_ref, dwb_ref, wdown_ref, hbuf, act, tm):
    hid = FFN_HIDDEN
    halo = FFN_HALO
    gate0 = hid // LANES
    nch = 2 * LANES
    h = _rms(x, g).astype(BF16)

    def conv(p):
        w = dww_ref[:, p * LANES:(p + 1) * LANES]
        y = dwb_ref[:, p * LANES:(p + 1) * LANES] + w[0:1] * hbuf[p, halo - 2:halo - 2 + tm, :]
        y = y + w[1:2] * hbuf[p, halo - 1:halo - 1 + tm, :]
        return y + w[2:3] * hbuf[p, halo:halo + tm, :]

    for c in range(0, hid, nch):
        for c0 in (c, hid + c):
            up = jnp.dot(h, wup_ref[:, c0:c0 + nch], preferred_element_type=F32)
            for i in range(nch // LANES):
                hbuf[c0 // LANES + i, halo:halo + tm, :] = up[:, i * LANES:(i + 1) * LANES]
        for p in range(c // LANES, (c + nch) // LANES):
            a = conv(p)
            gt = conv(gate0 + p)
            act[:, p * LANES:(p + 1) * LANES] = (a * (gt * jax.nn.sigmoid(gt))).astype(BF16)

    hbuf[:, 0:halo, :] = hbuf[:, tm:tm + halo, :]
    return x + jnp.dot(act[...], wdown_ref[...], preferred_element_type=F32)


def _odd_front(x, g, win_ref, bin_ref, ybuf, tm):
    c_in = CONV_INNER
    halo = CONV_HALO
    h = _rms(x, g).astype(BF16)
    nch = 256
    for c in range(0, c_in, nch):
        a = jnp.dot(h, win_ref[:, c:c + nch], preferred_element_type=F32) + bin_ref[:, c:c + nch]
        gate = (jnp.dot(h, win_ref[:, c_in + c:c_in + c + nch], preferred_element_type=F32)
                + bin_ref[:, c_in + c:c_in + c + nch])
        y = a * jax.nn.sigmoid(gate)
        for i in range(nch // LANES):
            ybuf[c // LANES + i, halo:halo + tm, :] = y[:, i * LANES:(i + 1) * LANES]


def _odd_back(x, dww_ref, dwb_ref, lng_ref, lnb_ref, wout_ref, bout_ref, ybuf, cbuf, tm):
    c_in = CONV_INNER
    halo = CONV_HALO
    rt = ODD_CONV_ROWS
    first = halo - (CONV_WIDTH - 1)
    for p in range(c_in // LANES):
        for r in range(0, tm, rt):
            acc = jnp.broadcast_to(dwb_ref[:, p * LANES:(p + 1) * LANES], (rt, LANES))
            for k in range(CONV_WIDTH):
                acc = acc + dww_ref[k:k + 1, p * LANES:(p + 1) * LANES] * ybuf[p, first + k + r:first + k + r + rt, :]
            cbuf[r:r + rt, p * LANES:(p + 1) * LANES] = acc

    ybuf[:, 0:halo, :] = ybuf[:, tm:tm + halo, :]

    y = cbuf[...]
    mu = jnp.mean(y, axis=-1, keepdims=True)
    yc = y - mu
    y = yc * lax.rsqrt(jnp.mean(yc * yc, axis=-1, keepdims=True) + EPS) * lng_ref[...] + lnb_ref[...]
    y = (y * jax.nn.sigmoid(y)).astype(BF16)
    return x + jnp.dot(y, wout_ref[...], preferred_element_type=F32) + bout_ref[...]


def _even_out_body(o_ref, u_ref, z_ref, x_ref, sgw_ref, sgb_ref, wout_ref, cat, tm):
    row = lax.broadcasted_iota(jnp.int32, (SG_CHUNK, SG_CHUNK), 0)
    col = lax.broadcasted_iota(jnp.int32, (SG_CHUNK, SG_CHUNK), 1)
    tri = col <= row
    cat[:, 0:DA_V_WIDTH] = o_ref[...]
    for g in range(SG_GROUPS):
        w = jnp.where(tri, sgw_ref[g], 0.0).astype(BF16)
        bias = sgb_ref[g]
        c0 = g * SG_GROUP_DIM
        for r in range(0, tm, 2 * SG_CHUNK):
            z2 = jnp.concatenate([z_ref[r + i * SG_CHUNK:r + (i + 1) * SG_CHUNK, c0:c0 + SG_GROUP_DIM]
                                  for i in range(2)], axis=1)
            zs2 = jnp.dot(w, z2, preferred_element_type=F32) + bias
            for i in range(2):
                ri = r + i * SG_CHUNK
                gate = u_ref[ri:ri + SG_CHUNK, c0:c0 + SG_GROUP_DIM] * zs2[:, i * SG_GROUP_DIM:(i + 1) * SG_GROUP_DIM]
                cat[ri:ri + SG_CHUNK, DA_V_WIDTH + c0:DA_V_WIDTH + c0 + SG_GROUP_DIM] = gate.astype(BF16)
    return x_ref[...] + jnp.dot(cat[...], wout_ref[...], preferred_element_type=F32)


N_MIXER_REFS = {"odd": 10, "even": 7}


def _tail_kernel(*refs, mode, tm, n_tiles, n_steps, final):
    n_mix = N_MIXER_REFS[mode]
    mix_refs = refs[:n_mix]
    fg_ref, wup_ref, fdw_ref, fdb_ref, wdown_ref, fing_ref, out_ref = refs[n_mix:n_mix + 7]
    x1buf, hbuf, act = refs[n_mix + 7:n_mix + 10]
    mix_scratch = refs[n_mix + 10:]
    g = pl.program_id(0)
    tile_a = jnp.minimum(g, n_steps - 1) % n_tiles
    tile_b = jnp.maximum(g - 1, 0) % n_tiles

    @pl.when(g == 0)
    def _():
        x1buf[...] = jnp.zeros(x1buf.shape, F32)

    @pl.when(tile_b == 0)
    def _():
        hbuf[:, 0:FFN_HALO, :] = jnp.zeros((hbuf.shape[0], FFN_HALO, LANES), F32)

    if mode == "odd":
        ybuf = mix_scratch[0]

        @pl.when(tile_a == 0)
        def _():
            ybuf[:, 0:CONV_HALO, :] = jnp.zeros((ybuf.shape[0], CONV_HALO, LANES), F32)

    if mode == "odd":
        x_ref, mg_ref, win_ref, bin_ref = mix_refs[:4]
        ybuf, cbuf = mix_scratch
        _odd_front(x_ref[...], mg_ref[...], win_ref, bin_ref, ybuf, tm)

    y = _ffn_body(x1buf[(g + 1) % 2], fg_ref[...], wup_ref, fdw_ref, fdb_ref, wdown_ref, hbuf, act, tm)
    if final:
        y = _rms(y, fing_ref[...])
    out_ref[...] = y

    if mode == "odd":
        x1 = _odd_back(x_ref[...], *mix_refs[4:], ybuf, cbuf, tm)
    else:
        x1 = _even_out_body(*mix_refs, *mix_scratch, tm)
    x1buf[g % 2] = x1


def _layer_tail(mode, l, mix_args, mix_specs, mix_scratch, shape, norm_ffn_g, ffn_w_up, ffn_dw_w,
                ffn_dw_b, ffn_w_down, final_norm_g):
    b, s, d = shape
    tm = TM_FFN
    hid = FFN_HIDDEN
    n_tiles = s // tm
    n_steps = b * n_tiles
    tile_b = lambda g: jnp.maximum(g - 1, 0)
    return pl.pallas_call(
        functools.partial(_tail_kernel, mode=mode, tm=tm, n_tiles=n_tiles, n_steps=n_steps,
                          final=(l == DEPTH - 1)),
        grid=(n_steps + 1,),
        in_specs=mix_specs + [
            _layer_spec(_rows(norm_ffn_g), l),
            _layer_spec(ffn_w_up, l, buffers=1),
            _layer_spec(ffn_dw_w, l),
            _layer_spec(_rows(ffn_dw_b), l),
            _layer_spec(ffn_w_down, l, buffers=1),
            _const_spec((1, d)),
        ],
        out_specs=pl.BlockSpec((None, tm, d), lambda g: (tile_b(g) // n_tiles, tile_b(g) % n_tiles, 0)),
        out_shape=jax.ShapeDtypeStruct((b, s, d), F32),
        scratch_shapes=[
            pltpu.VMEM((2, tm, d), F32),
            pltpu.VMEM((2 * hid // LANES, tm + FFN_HALO, LANES), F32),
            pltpu.VMEM((tm, hid), BF16),
        ] + mix_scratch,
        compiler_params=_params(1),
        name=mode + "_tail",
    )(*mix_args, _rows(norm_ffn_g), ffn_w_up, ffn_dw_w, _rows(ffn_dw_b), ffn_w_down,
      final_norm_g.reshape(1, d))


def _tile_a_spec(width, n_tiles, n_steps, perm=None):
    def index(g):
        t = jnp.minimum(g, n_steps - 1)
        j = t % n_tiles
        return (t // n_tiles, j if perm is None else perm(j, n_tiles), 0)
    return pl.BlockSpec((None, TM_FFN, width), index)


def _odd_layer(x, l, norm_mix_g, conv_w_in, conv_b_in, conv_dw_w, conv_dw_b, conv_ln_g, conv_ln_b,
               conv_w_out, conv_b_out, *ffn_args):
    b, s, d = x.shape
    i = l // 2
    c_in = CONV_INNER
    n_tiles = s // TM_FFN
    specs = [
        _tile_a_spec(d, n_tiles, b * n_tiles),
        _layer_spec(_rows(norm_mix_g), l),
        _layer_spec(conv_w_in, i, buffers=1),
        _layer_spec(_rows(conv_b_in), i),
        _layer_spec(conv_dw_w, i),
        _layer_spec(_rows(conv_dw_b), i),
        _layer_spec(_rows(conv_ln_g), i),
        _layer_spec(_rows(conv_ln_b), i),
        _layer_spec(conv_w_out, i, buffers=1),
        _layer_spec(_rows(conv_b_out), i),
    ]
    args = [x, _rows(norm_mix_g), conv_w_in, _rows(conv_b_in), conv_dw_w, _rows(conv_dw_b),
            _rows(conv_ln_g), _rows(conv_ln_b), conv_w_out, _rows(conv_b_out)]
    scratch = [
        pltpu.VMEM((c_in // LANES, TM_FFN + CONV_HALO, LANES), F32),
        pltpu.VMEM((TM_FFN, c_in), F32),
    ]
    return _layer_tail("odd", l, args, specs, scratch, x.shape, *ffn_args)


def _even_tail(o, u, z, x, l, sg_w, sg_b, ab_w_out, *ffn_args):
    b, s, d = x.shape
    i = l // 2
    n_tiles = s // TM_FFN
    n_steps = b * n_tiles
    sg_b = sg_b.reshape(sg_b.shape + (1,))
    specs = [
        _tile_a_spec(DA_V_WIDTH, n_tiles, n_steps, perm=_pair_perm),
        _tile_a_spec(SG_WIDTH, n_tiles, n_steps),
        _tile_a_spec(SG_WIDTH, n_tiles, n_steps),
        _tile_a_spec(d, n_tiles, n_steps),
        _layer_spec(sg_w, i),
        _layer_spec(sg_b, i),
        _layer_spec(ab_w_out, i, buffers=1),
    ]
    scratch = [pltpu.VMEM((TM_FFN, AB_OUT_WIDTH), BF16)]
    return _layer_tail("even", l, [o, u, z, x, sg_w, sg_b, ab_w_out], specs, scratch, x.shape, *ffn_args)


def _gelu_tanh(x):
    return 0.5 * x * (1.0 + jnp.tanh(math.sqrt(2.0 / math.pi) * (x + 0.044715 * (x * x * x))))


def _even_in_kernel(x_ref, g_ref, w_ref, cos_ref, sin_ref, lng_ref, lnb_ref,
                    q_ref, k_ref, v_ref, u_ref, z_ref, *, tm):
    x = x_ref[...]
    h = _rms(x, g_ref[...]).astype(BF16)
    cos = cos_ref[...]
    sin = sin_ref[...]
    lane = lax.broadcasted_iota(jnp.int32, (tm, LANES), 1)
    lower = (lane % DA_HEAD_DIM) < (DA_HEAD_DIM // 2)
    half = DA_HEAD_DIM // 2

    def rope_store(dst_ref, col0, scale):
        t_all = jnp.dot(h, w_ref[:, col0:col0 + DA_QK_WIDTH], preferred_element_type=F32)
        cos_s = cos * scale
        sin_s = sin * scale
        for j in range(0, DA_QK_WIDTH, LANES):
            t = t_all[:, j:j + LANES]
            rot = jnp.where(lower, pltpu.roll(t, LANES - half, axis=1), pltpu.roll(t, half, axis=1))
            dst_ref[:, j:j + LANES] = (t * cos_s + rot * sin_s).astype(BF16)

    o1 = 2 * DA_QK_WIDTH
    o2 = o1 + DA_V_WIDTH
    o3 = o2 + SG_WIDTH
    z_all = _gelu_tanh(jnp.dot(h, w_ref[:, o3:o3 + SG_WIDTH], preferred_element_type=F32))
    for j in range(0, SG_WIDTH, SG_GROUP_DIM):
        z = z_all[:, j:j + SG_GROUP_DIM]
        mu = jnp.mean(z, axis=-1, keepdims=True)
        zc = z - mu
        z = zc * lax.rsqrt(jnp.mean(zc * zc, axis=-1, keepdims=True) + EPS)
        z = z * lng_ref[:, j:j + SG_GROUP_DIM] + lnb_ref[:, j:j + SG_GROUP_DIM]
        z_ref[:, j:j + SG_GROUP_DIM] = z.astype(BF16)
    u_ref[...] = _gelu_tanh(jnp.dot(h, w_ref[:, o2:o2 + SG_WIDTH], preferred_element_type=F32))
    rope_store(q_ref, 0, DA_HEAD_DIM ** -0.5 * math.log2(math.e))
    rope_store(k_ref, DA_QK_WIDTH, 1.0)
    v_ref[...] = jnp.dot(h, w_ref[:, o1:o1 + DA_V_WIDTH], preferred_element_type=F32).astype(BF16)


def _even_in(x, l, norm_mix_g, ab_w_in, cos, sin, sg_ln_g, sg_ln_b):
    b, s, d = x.shape
    i = l // 2
    tm = TM_IN
    n_tiles = s // tm
    row_spec = lambda w: pl.BlockSpec((None, tm, w), lambda i, j: (i, j, 0))
    q_spec = pl.BlockSpec((None, tm, DA_QK_WIDTH), lambda i, j: (i, _pair_perm(j, n_tiles), 0))
    return pl.pallas_call(
        functools.partial(_even_in_kernel, tm=tm),
        grid=(b, n_tiles),
        in_specs=[
            row_spec(d),
            _layer_spec(_rows(norm_mix_g), l),
            _layer_spec(ab_w_in, i),
            pl.BlockSpec((tm, LANES), lambda i, j: (j, 0)),
            pl.BlockSpec((tm, LANES), lambda i, j: (j, 0)),
            _layer_spec(_rows(sg_ln_g), i),
            _layer_spec(_rows(sg_ln_b), i),
        ],
        out_specs=[q_spec, row_spec(DA_QK_WIDTH), row_spec(DA_V_WIDTH),
                   row_spec(SG_WIDTH), row_spec(SG_WIDTH)],
        out_shape=[
            jax.ShapeDtypeStruct((b, s, DA_QK_WIDTH), BF16),
            jax.ShapeDtypeStruct((b, s, DA_QK_WIDTH), BF16),
            jax.ShapeDtypeStruct((b, s, DA_V_WIDTH), BF16),
            jax.ShapeDtypeStruct((b, s, SG_WIDTH), F32),
            jax.ShapeDtypeStruct((b, s, SG_WIDTH), BF16),
        ],
        compiler_params=_params(2),
        name="even_in",
    )(x, _rows(norm_mix_g), ab_w_in, cos, sin, _rows(sg_ln_g), _rows(sg_ln_b))


def _pair_perm(j, n_tiles):
    return jnp.where(j < n_tiles // 2, 2 * j, 2 * (n_tiles - 1 - j) + 1)


def _attn_kernel(lam_ref, q_ref, k_ref, v_ref, sg_ref, o_ref, qs_ref, m_ref, acc_ref, bias_ref,
                 *, tq, n_tiles, lambda_init):
    pair = pl.program_id(1)
    rows = 2 * tq
    width = 2 * tq
    last = n_tiles - 1
    nt = (((1,), (1,)), ((), ()))
    lane = lax.broadcasted_iota(jnp.int32, (tq, LANES), 1)
    ones = jnp.ones((width, LANES), BF16)

    def scores(sel, h, start):
        k = k_ref[pl.ds(start, width), h * LANES:(h + 1) * LANES]
        return lax.dot_general(qs_ref[sel, h], k, nt, preferred_element_type=F32)

    def values(h, start):
        return jnp.concatenate([v_ref[pl.ds(start, width), h * LANES:(h + 1) * LANES], ones], axis=1)

    def spread(m):
        return jnp.concatenate([m] * (width // LANES), axis=1)

    for t, tile in enumerate((pair, last - pair)):
        blk = tile // 2
        start = pl.multiple_of(blk * width, width)
        r = lax.broadcasted_iota(jnp.int32, (tq, width), 0)
        c = lax.broadcasted_iota(jnp.int32, (tq, width), 1)
        bias = jnp.where(c <= r + (tile * tq - blk * width), 0.0, -jnp.inf).astype(F32)
        bias_ref[t] = bias
        for h in range(DA_HEADS):
            qh = q_ref[t * tq:(t + 1) * tq, h * LANES:(h + 1) * LANES]
            zero = jnp.zeros_like(qh)
            qs_ref[t, h, 0:tq, :] = jnp.where(lane < DA_HEAD_DIM, qh, zero)
            qs_ref[t, h, tq:rows, :] = jnp.where(lane >= DA_HEAD_DIM, qh, zero)
            s = scores(t, h, start) + jnp.concatenate([bias_ref[t]] * 2, axis=0)
            m = jnp.broadcast_to(jnp.max(s, axis=-1, keepdims=True), (rows, LANES))
            p = jnp.exp2(s - spread(m)).astype(BF16)
            acc_ref[t, h] = jnp.dot(p, values(h, start), preferred_element_type=F32)
            m_ref[t, h] = m

    def full_block(sel, start):
        for h in range(DA_HEADS):
            s = scores(sel, h, start)
            m_old = m_ref[sel, h]
            m_new = jnp.maximum(m_old, jnp.max(s, axis=-1, keepdims=True))
            alpha = jnp.exp2(m_old - m_new)
            p = jnp.exp2(s - spread(m_new)).astype(BF16)
            pv = jnp.dot(p, values(h, start), preferred_element_type=F32)
            acc_ref[sel, h] = jnp.concatenate([alpha, alpha], axis=1) * acc_ref[sel, h] + pv
            m_ref[sel, h] = m_new

    lp = lam_ref[...]
    lam = (jnp.exp(jnp.sum(lp[0:1] * lp[1:2], axis=-1, keepdims=True))
           - jnp.exp(jnp.sum(lp[2:3] * lp[3:4], axis=-1, keepdims=True)) + lambda_init)

    def finish(t):
        for h in range(DA_HEADS):
            o1 = acc_ref[t, h, 0:tq, 0:LANES] / acc_ref[t, h, 0:tq, LANES:2 * LANES]
            o2 = acc_ref[t, h, tq:rows, 0:LANES] / acc_ref[t, h, tq:rows, LANES:2 * LANES]
            o = _rms(o1 - lam * o2, sg_ref[...]) * (1.0 - lambda_init)
            o_ref[t * tq:(t + 1) * tq, h * LANES:(h + 1) * LANES] = o.astype(BF16)

    full_a = pair // 2
    full_block(jnp.where(full_a > 0, 0, 1), 0)
    finish(0)
    for t in range(1, (n_tiles - 2) // 2):
        full_block(1, pl.multiple_of((t - full_a) * width, width))
    finish(1)


def _diff_attention(l, lam_params, q, k, v, subln_g, lambda_init):
    b, s, w = q.shape
    tq = TQ
    n_tiles = s // tq
    assert n_tiles % 4 == 0 and (n_tiles // 2 - 1) // 2 <= 1, "block schedule assumes n_tiles in (4, 8)"
    return pl.pallas_call(
        functools.partial(_attn_kernel, tq=tq, n_tiles=n_tiles, lambda_init=lambda_init),
        grid=(b, n_tiles // 2),
        in_specs=[
            _layer_spec(lam_params, l // 2),
            pl.BlockSpec((None, 2 * tq, w), lambda i, j: (i, j, 0)),
            pl.BlockSpec((None, s, w), lambda i, j: (i, 0, 0)),
            pl.BlockSpec((None, s, w), lambda i, j: (i, 0, 0)),
            _layer_spec(_rows(subln_g), l // 2),
        ],
        out_specs=pl.BlockSpec((None, 2 * tq, w), lambda i, j: (i, j, 0)),
        out_shape=jax.ShapeDtypeStruct((b, s, DA_V_WIDTH), BF16),
        scratch_shapes=[
            pltpu.VMEM((2, DA_HEADS, 2 * tq, LANES), BF16),
            pltpu.VMEM((2, DA_HEADS, 2 * tq, LANES), F32),
            pltpu.VMEM((2, DA_HEADS, 2 * tq, 2 * LANES), F32),
            pltpu.VMEM((2, tq, 2 * tq), F32),
        ],
        compiler_params=_params(2),
        name="diff_attn",
    )(lam_params, q, k, v, _rows(subln_g))


def _rope_tables(seq):
    dim = DA_HEAD_DIM
    inv = 1.0 / (ROPE_THETA ** (jnp.arange(0, dim, 2, dtype=F32) / dim))
    ang = jnp.arange(seq, dtype=F32)[:, None] * inv[None, :]
    ang = jnp.concatenate([ang, ang], axis=-1)
    cos = jnp.cos(ang)
    sin = jnp.sin(ang)
    sign = jnp.where(jnp.arange(dim) < dim // 2, -1.0, 1.0).astype(F32)
    reps = LANES // dim
    return jnp.tile(cos, (1, reps)), jnp.tile(sin * sign[None, :], (1, reps))


def kernel(x, norm_mix_g, norm_ffn_g, ab_w_in, ab_w_out, diff_lq1, diff_lk1, diff_lq2, diff_lk2,
           diff_subln_g, sg_ln_g, sg_ln_b, sg_w, sg_b, conv_w_in, conv_b_in, conv_dw_w, conv_dw_b,
           conv_ln_g, conv_ln_b, conv_w_out, conv_b_out, ffn_w_up, ffn_dw_w, ffn_dw_b, ffn_w_down,
           final_norm_g):
    seq = x.shape[1]
    cos, sin = _rope_tables(seq)
    ab_w_in = ab_w_in.astype(BF16)
    ab_w_out = ab_w_out.astype(BF16)
    conv_w_in = conv_w_in.astype(BF16)
    conv_w_out = conv_w_out.astype(BF16)
    ffn_w_up = ffn_w_up.astype(BF16)
    ffn_w_down = ffn_w_down.astype(BF16)
    lam_params = jnp.stack([diff_lq1, diff_lk1, diff_lq2, diff_lk2], axis=1)
    ffn_args = (norm_ffn_g, ffn_w_up, ffn_dw_w, ffn_dw_b, ffn_w_down, final_norm_g)
    for l in range(DEPTH):
        if l % 2 == 0:
            lambda_init = 0.8 - 0.6 * math.exp(-0.3 * l)
            q, k, v, u, z = _even_in(x, l, norm_mix_g, ab_w_in, cos, sin, sg_ln_g, sg_ln_b)
            o = _diff_attention(l, lam_params, q, k, v, diff_subln_g, lambda_init)
            x = _even_tail(o, u, z, x, l, sg_w, sg_b, ab_w_out, *ffn_args)
        else:
            x = _odd_layer(x, l, norm_mix_g, conv_w_in, conv_b_in, conv_dw_w, conv_dw_b, conv_ln_g,
                           conv_ln_b, conv_w_out, conv_b_out, *ffn_args)
    return x
```

```python
import functools
import math

import jax
import jax.numpy as jnp
from jax import lax
from jax.experimental import pallas as pl
from jax.experimental.pallas import tpu as pltpu

D_MODEL = 1024
DEPTH = 4
EPS = 1e-6

DA_HEADS = 4
DA_HEAD_DIM = 64
DA_V_DIM = 2 * DA_HEAD_DIM
DA_QK_WIDTH = DA_HEADS * 2 * DA_HEAD_DIM
DA_V_WIDTH = DA_HEADS * DA_V_DIM
ROPE_THETA = 10000.0

SG_GROUPS = 4
SG_CHUNK = 128
SG_GROUP_DIM = 128
SG_WIDTH = SG_GROUPS * SG_GROUP_DIM

AB_IN_WIDTH = 2 * DA_QK_WIDTH + DA_V_WIDTH + 2 * SG_WIDTH
AB_OUT_WIDTH = DA_V_WIDTH + SG_WIDTH

CONV_INNER = D_MODEL
CONV_WIDTH = 31
FFN_HIDDEN = 2816
FFN_CONV_WIDTH = 3

LANES = 128
SUBLANES = 8
VMEM_LIMIT = 56 * 1024 * 1024

F32 = jnp.float32
BF16 = jnp.bfloat16

TQ = 256
TM_IN = 2 * TQ
TM_FFN = TQ
ODD_CONV_ROWS = 64
CONV_HALO = 32
FFN_HALO = SUBLANES


def _rms(x, g):
    return x * lax.rsqrt(jnp.mean(x * x, axis=-1, keepdims=True) + EPS) * g


def _params(n_axes):
    return pltpu.CompilerParams(
        dimension_semantics=("arbitrary",) * n_axes, vmem_limit_bytes=VMEM_LIMIT)


def _const_spec(shape):
    nd = len(shape)
    return pl.BlockSpec(shape, lambda *_: (0,) * nd)


def _layer_spec(arr, l, buffers=None):
    tail = arr.shape[1:]
    zeros = (0,) * len(tail)
    kw = {} if buffers is None else {"pipeline_mode": pl.Buffered(buffers)}
    return pl.BlockSpec((None,) + tail, lambda *_: (l,) + zeros, **kw)


def _rows(a):
    return a.reshape(a.shape[0], 1, a.shape[1])


def _ffn_body(x, g, wup_ref, dww_ref, dwb_ref, wdown_ref, hbuf, act, tm):
    hid = FFN_HIDDEN
    halo = FFN_HALO
    gate0 = hid // LANES
    nch = 2 * LANES
    h = _rms(x, g).astype(BF16)

    def conv(p):
        w = dww_ref[:, p * LANES:(p + 1) * LANES]
        y = dwb_ref[:, p * LANES:(p + 1) * LANES] + w[0:1] * hbuf[p, halo - 2:halo - 2 + tm, :]
        y = y + w[1:2] * hbuf[p, halo - 1:halo - 1 + tm, :]
        return y + w[2:3] * hbuf[p, halo:halo + tm, :]

    for c in range(0, hid, nch):
        for c0 in (c, hid + c):
            up = jnp.dot(h, wup_ref[:, c0:c0 + nch], preferred_element_type=F32)
            for i in range(nch // LANES):
                hbuf[c0 // LANES + i, halo:halo + tm, :] = up[:, i * LANES:(i + 1) * LANES]
        for p in range(c // LANES, (c + nch) // LANES):
            a = conv(p)
            gt = conv(gate0 + p)
            act[:, p * LANES:(p + 1) * LANES] = (a * (gt * jax.nn.sigmoid(gt))).astype(BF16)

    hbuf[:, 0:halo, :] = hbuf[:, tm:tm + halo, :]
    return x + jnp.dot(act[...], wdown_ref[...], preferred_element_type=F32)


def _odd_front(x, g, win_ref, bin_ref, ybuf, tm):
    c_in = CONV_INNER
    halo = CONV_HALO
    h = _rms(x, g).astype(BF16)
    nch = 256
    for c in range(0, c_in, nch):
        a = jnp.dot(h, win_ref[:, c:c + nch], preferred_element_type=F32) + bin_ref[:, c:c + nch]
        gate = (jnp.dot(h, win_ref[:, c_in + c:c_in + c + nch], preferred_element_type=F32)
                + bin_ref[:, c_in + c:c_in + c + nch])
        y = a * jax.nn.sigmoid(gate)
        for i in range(nch // LANES):
            ybuf[c // LANES + i, halo:halo + tm, :] = y[:, i * LANES:(i + 1) * LANES]


def _odd_back(x, dww_ref, dwb_ref, lng_ref, lnb_ref, wout_ref, bout_ref, ybuf, cbuf, tm):
    c_in = CONV_INNER
    halo = CONV_HALO
    rt = ODD_CONV_ROWS
    first = halo - (CONV_WIDTH - 1)
    for p in range(c_in // LANES):
        for r in range(0, tm, rt):
            acc = jnp.broadcast_to(dwb_ref[:, p * LANES:(p + 1) * LANES], (rt, LANES))
            for k in range(CONV_WIDTH):
                acc = acc + dww_ref[k:k + 1, p * LANES:(p + 1) * LANES] * ybuf[p, first + k + r:first + k + r + rt, :]
            cbuf[r:r + rt, p * LANES:(p + 1) * LANES] = acc

    ybuf[:, 0:halo, :] = ybuf[:, tm:tm + halo, :]

    y = cbuf[...]
    mu = jnp.mean(y, axis=-1, keepdims=True)
    yc = y - mu
    y = yc * lax.rsqrt(jnp.mean(yc * yc, axis=-1, keepdims=True) + EPS) * lng_ref[...] + lnb_ref[...]
    y = (y * jax.nn.sigmoid(y)).astype(BF16)
    return x + jnp.dot(y, wout_ref[...], preferred_element_type=F32) + bout_ref[...]


def _even_out_body(o_ref, u_ref, z_ref, x_ref, sgw_ref, sgb_ref, wout_ref, cat, tm):
    row = lax.broadcasted_iota(jnp.int32, (SG_CHUNK, SG_CHUNK), 0)
    col = lax.broadcasted_iota(jnp.int32, (SG_CHUNK, SG_CHUNK), 1)
    tri = col <= row
    cat[:, 0:DA_V_WIDTH] = o_ref[...]
    for g in range(SG_GROUPS):
        w = jnp.where(tri, sgw_ref[g], 0.0).astype(BF16)
        bias = sgb_ref[g]
        c0 = g * SG_GROUP_DIM
        for r in range(0, tm, 2 * SG_CHUNK):
            z2 = jnp.concatenate([z_ref[r + i * SG_CHUNK:r + (i + 1) * SG_CHUNK, c0:c0 + SG_GROUP_DIM]
                                  for i in range(2)], axis=1)
            zs2 = jnp.dot(w, z2, preferred_element_type=F32) + bias
            for i in range(2):
                ri = r + i * SG_CHUNK
                gate = u_ref[ri:ri + SG_CHUNK, c0:c0 + SG_GROUP_DIM] * zs2[:, i * SG_GROUP_DIM:(i + 1) * SG_GROUP_DIM]
                cat[ri:ri + SG_CHUNK, DA_V_WIDTH + c0:DA_V_WIDTH + c0 + SG_GROUP_DIM] = gate.astype(BF16)
    return x_ref[...] + jnp.dot(cat[...], wout_ref[...], preferred_element_type=F32)


N_MIXER_REFS = {"odd": 10, "even": 7}


def _tail_kernel(*refs, mode, tm, n_tiles, n_steps, final):
    n_mix = N_MIXER_REFS[mode]
    mix_refs = refs[:n_mix]
    fg_ref, wup_ref, fdw_ref, fdb_ref, wdown_ref, fing_ref, out_ref = refs[n_mix:n_mix + 7]
    x1buf, hbuf, act = refs[n_mix + 7:n_mix + 10]
    mix_scratch = refs[n_mix + 10:]
    g = pl.program_id(0)
    tile_a = jnp.minimum(g, n_steps - 1) % n_tiles
    tile_b = jnp.maximum(g - 1, 0) % n_tiles

    @pl.when(g == 0)
    def _():
        x1buf[...] = jnp.zeros(x1buf.shape, F32)

    @pl.when(tile_b == 0)
    def _():
        hbuf[:, 0:FFN_HALO, :] = jnp.zeros((hbuf.shape[0], FFN_HALO, LANES), F32)

    if mode == "odd":
        ybuf = mix_scratch[0]

        @pl.when(tile_a == 0)
        def _():
            ybuf[:, 0:CONV_HALO, :] = jnp.zeros((ybuf.shape[0], CONV_HALO, LANES), F32)

    if mode == "odd":
        x_ref, mg_ref, win_ref, bin_ref = mix_refs[:4]
        ybuf, cbuf = mix_scratch
        _odd_front(x_ref[...], mg_ref[...], win_ref, bin_ref, ybuf, tm)

    y = _ffn_body(x1buf[(g + 1) % 2], fg_ref[...], wup_ref, fdw_ref, fdb_ref, wdown_ref, hbuf, act, tm)
    if final:
        y = _rms(y, fing_ref[...])
    out_ref[...] = y

    if mode == "odd":
        x1 = _odd_back(x_ref[...], *mix_refs[4:], ybuf, cbuf, tm)
    else:
        x1 = _even_out_body(*mix_refs, *mix_scratch, tm)
    x1buf[g % 2] = x1


def _layer_tail(mode, l, mix_args, mix_specs, mix_scratch, shape, norm_ffn_g, ffn_w_up, ffn_dw_w,
                ffn_dw_b, ffn_w_down, final_norm_g):
    b, s, d = shape
    tm = TM_FFN
    hid = FFN_HIDDEN
    n_tiles = s // tm
    n_steps = b * n_tiles
    tile_b = lambda g: jnp.maximum(g - 1, 0)
    return pl.pallas_call(
        functools.partial(_tail_kernel, mode=mode, tm=tm, n_tiles=n_tiles, n_steps=n_steps,
                          final=(l == DEPTH - 1)),
        grid=(n_steps + 1,),
        in_specs=mix_specs + [
            _layer_spec(_rows(norm_ffn_g), l),
            _layer_spec(ffn_w_up, l, buffers=1),
            _layer_spec(ffn_dw_w, l),
            _layer_spec(_rows(ffn_dw_b), l),
            _layer_spec(ffn_w_down, l, buffers=1),
            _const_spec((1, d)),
        ],
        out_specs=pl.BlockSpec((None, tm, d), lambda g: (tile_b(g) // n_tiles, tile_b(g) % n_tiles, 0)),
        out_shape=jax.ShapeDtypeStruct((b, s, d), F32),
        scratch_shapes=[
            pltpu.VMEM((2, tm, d), F32),
            pltpu.VMEM((2 * hid // LANES, tm + FFN_HALO, LANES), F32),
            pltpu.VMEM((tm, hid), BF16),
        ] + mix_scratch,
        compiler_params=_params(1),
        name=mode + "_tail",
    )(*mix_args, _rows(norm_ffn_g), ffn_w_up, ffn_dw_w, _rows(ffn_dw_b), ffn_w_down,
      final_norm_g.reshape(1, d))


def _tile_a_spec(width, n_tiles, n_steps, perm=None):
    def index(g):
        t = jnp.minimum(g, n_steps - 1)
        j = t % n_tiles
        return (t // n_tiles, j if perm is None else perm(j, n_tiles), 0)
    return pl.BlockSpec((None, TM_FFN, width), index)


def _odd_layer(x, l, norm_mix_g, conv_w_in, conv_b_in, conv_dw_w, conv_dw_b, conv_ln_g, conv_ln_b,
               conv_w_out, conv_b_out, *ffn_args):
    b, s, d = x.shape
    i = l // 2
    c_in = CONV_INNER
    n_tiles = s // TM_FFN
    specs = [
        _tile_a_spec(d, n_tiles, b * n_tiles),
        _layer_spec(_rows(norm_mix_g), l),
        _layer_spec(conv_w_in, i, buffers=1),
        _layer_spec(_rows(conv_b_in), i),
        _layer_spec(conv_dw_w, i),
        _layer_spec(_rows(conv_dw_b), i),
        _layer_spec(_rows(conv_ln_g), i),
        _layer_spec(_rows(conv_ln_b), i),
        _layer_spec(conv_w_out, i, buffers=1),
        _layer_spec(_rows(conv_b_out), i),
    ]
    args = [x, _rows(norm_mix_g), conv_w_in, _rows(conv_b_in), conv_dw_w, _rows(conv_dw_b),
            _rows(conv_ln_g), _rows(conv_ln_b), conv_w_out, _rows(conv_b_out)]
    scratch = [
        pltpu.VMEM((c_in // LANES, TM_FFN + CONV_HALO, LANES), F32),
        pltpu.VMEM((TM_FFN, c_in), F32),
    ]
    return _layer_tail("odd", l, args, specs, scratch, x.shape, *ffn_args)


def _even_tail(o, u, z, x, l, sg_w, sg_b, ab_w_out, *ffn_args):
    b, s, d = x.shape
    i = l // 2
    n_tiles = s // TM_FFN
    n_steps = b * n_tiles
    sg_b = sg_b.reshape(sg_b.shape + (1,))
    specs = [
        _tile_a_spec(DA_V_WIDTH, n_tiles, n_steps, perm=_pair_perm),
        _tile_a_spec(SG_WIDTH, n_tiles, n_steps),
        _tile_a_spec(SG_WIDTH, n_tiles, n_steps),
        _tile_a_spec(d, n_tiles, n_steps),
        _layer_spec(sg_w, i),
        _layer_spec(sg_b, i),
        _layer_spec(ab_w_out, i, buffers=1),
    ]
    scratch = [pltpu.VMEM((TM_FFN, AB_OUT_WIDTH), BF16)]
    return _layer_tail("even", l, [o, u, z, x, sg_w, sg_b, ab_w_out], specs, scratch, x.shape, *ffn_args)


def _gelu_tanh(x):
    return 0.5 * x * (1.0 + jnp.tanh(math.sqrt(2.0 / math.pi) * (x + 0.044715 * (x * x * x))))


def _even_in_kernel(x_ref, g_ref, w_ref, cos_ref, sin_ref, lng_ref, lnb_ref,
                    q_ref, k_ref, v_ref, u_ref, z_ref, *, tm):
    x = x_ref[...]
    h = _rms(x, g_ref[...]).astype(BF16)
    cos = cos_ref[...]
    sin = sin_ref[...]
    lane = lax.broadcasted_iota(jnp.int32, (tm, LANES), 1)
    lower = (lane % DA_HEAD_DIM) < (DA_HEAD_DIM // 2)
    half = DA_HEAD_DIM // 2

    def rope_store(dst_ref, col0, scale):
        t_all = jnp.dot(h, w_ref[:, col0:col0 + DA_QK_WIDTH], preferred_element_type=F32)
        cos_s = cos * scale
        sin_s = sin * scale
        for j in range(0, DA_QK_WIDTH, LANES):
            t = t_all[:, j:j + LANES]
            rot = jnp.where(lower, pltpu.roll(t, LANES - half, axis=1), pltpu.roll(t, half, axis=1))
            dst_ref[:, j:j + LANES] = (t * cos_s + rot * sin_s).astype(BF16)

    o1 = 2 * DA_QK_WIDTH
    o2 = o1 + DA_V_WIDTH
    o3 = o2 + SG_WIDTH
    z_all = _gelu_tanh(jnp.dot(h, w_ref[:, o3:o3 + SG_WIDTH], preferred_element_type=F32))
    for j in range(0, SG_WIDTH, SG_GROUP_DIM):
        z = z_all[:, j:j + SG_GROUP_DIM]
        mu = jnp.mean(z, axis=-1, keepdims=True)
        zc = z - mu
        z = zc * lax.rsqrt(jnp.mean(zc * zc, axis=-1, keepdims=True) + EPS)
        z = z * lng_ref[:, j:j + SG_GROUP_DIM] + lnb_ref[:, j:j + SG_GROUP_DIM]
        z_ref[:, j:j + SG_GROUP_DIM] = z.astype(BF16)
    u_ref[...] = _gelu_tanh(jnp.dot(h, w_ref[:, o2:o2 + SG_WIDTH], preferred_element_type=F32))
    rope_store(q_ref, 0, DA_HEAD_DIM ** -0.5 * math.log2(math.e))
    rope_store(k_ref, DA_QK_WIDTH, 1.0)
    v_ref[...] = jnp.dot(h, w_ref[:, o1:o1 + DA_V_WIDTH], preferred_element_type=F32).astype(BF16)


def _even_in(x, l, norm_mix_g, ab_w_in, cos, sin, sg_ln_g, sg_ln_b):
    b, s, d = x.shape
    i = l // 2
    tm = TM_IN
    n_tiles = s // tm
    row_spec = lambda w: pl.BlockSpec((None, tm, w), lambda i, j: (i, j, 0))
    return pl.pallas_call(
        functools.partial(_even_in_kernel, tm=tm),
        grid=(b, n_tiles),
        in_specs=[
            row_spec(d),
            _layer_spec(_rows(norm_mix_g), l),
            _layer_spec(ab_w_in, i),
            pl.BlockSpec((tm, LANES), lambda i, j: (j, 0)),
            pl.BlockSpec((tm, LANES), lambda i, j: (j, 0)),
            _layer_spec(_rows(sg_ln_g), i),
            _layer_spec(_rows(sg_ln_b), i),
        ],
        out_specs=[row_spec(DA_QK_WIDTH), row_spec(DA_QK_WIDTH), row_spec(DA_V_WIDTH),
                   row_spec(SG_WIDTH), row_spec(SG_WIDTH)],
        out_shape=[
            jax.ShapeDtypeStruct((b, s, DA_QK_WIDTH), BF16),
            jax.ShapeDtypeStruct((b, s, DA_QK_WIDTH), BF16),
            jax.ShapeDtypeStruct((b, s, DA_V_WIDTH), BF16),
            jax.ShapeDtypeStruct((b, s, SG_WIDTH), F32),
            jax.ShapeDtypeStruct((b, s, SG_WIDTH), BF16),
        ],
        compiler_params=_params(2),
        name="even_in",
    )(x, _rows(norm_mix_g), ab_w_in, cos, sin, _rows(sg_ln_g), _rows(sg_ln_b))


def _pair_perm(j, n_tiles):
    return jnp.where(j < n_tiles // 2, 2 * j, 2 * (n_tiles - 1 - j) + 1)


def _attn_kernel(lam_ref, qa_ref, qb_ref, k_ref, v_ref, sg_ref, o_ref, qs_ref, m_ref, acc_ref, bias_ref,
                 *, tq, n_tiles, lambda_init):
    pair = pl.program_id(1)
    rows = 2 * tq
    width = 2 * tq
    last = n_tiles - 1
    nt = (((1,), (1,)), ((), ()))
    lane = lax.broadcasted_iota(jnp.int32, (tq, LANES), 1)
    ones = jnp.ones((width, LANES), BF16)

    def scores(sel, h, start):
        k = k_ref[pl.ds(start, width), h * LANES:(h + 1) * LANES]
        return lax.dot_general(qs_ref[sel, h], k, nt, preferred_element_type=F32)

    def values(h, start):
        return jnp.concatenate([v_ref[pl.ds(start, width), h * LANES:(h + 1) * LANES], ones], axis=1)

    def spread(m):
        return jnp.concatenate([m] * (width // LANES), axis=1)

    for t, tile in enumerate((pair, last - pair)):
        blk = tile // 2
        start = pl.multiple_of(blk * width, width)
        r = lax.broadcasted_iota(jnp.int32, (tq, width), 0)
        c = lax.broadcasted_iota(jnp.int32, (tq, width), 1)
        bias = jnp.where(c <= r + (tile * tq - blk * width), 0.0, -jnp.inf).astype(F32)
        bias_ref[t] = bias
        for h in range(DA_HEADS):
            qh = (qa_ref, qb_ref)[t][:, h * LANES:(h + 1) * LANES]
            zero = jnp.zeros_like(qh)
            qs_ref[t, h, 0:tq, :] = jnp.where(lane < DA_HEAD_DIM, qh, zero)
            qs_ref[t, h, tq:rows, :] = jnp.where(lane >= DA_HEAD_DIM, qh, zero)
            s = scores(t, h, start) + jnp.concatenate([bias_ref[t]] * 2, axis=0)
            m = jnp.broadcast_to(jnp.max(s, axis=-1, keepdims=True), (rows, LANES))
            p = jnp.exp2(s - spread(m)).astype(BF16)
            acc_ref[t, h] = jnp.dot(p, values(h, start), preferred_element_type=F32)
            m_ref[t, h] = m

    def full_block(sel, start):
        for h in range(DA_HEADS):
            s = scores(sel, h, start)
            m_old = m_ref[sel, h]
            m_new = jnp.maximum(m_old, jnp.max(s, axis=-1, keepdims=True))
            alpha = jnp.exp2(m_old - m_new)
            p = jnp.exp2(s - spread(m_new)).astype(BF16)
            pv = jnp.dot(p, values(h, start), preferred_element_type=F32)
            acc_ref[sel, h] = jnp.concatenate([alpha, alpha], axis=1) * acc_ref[sel, h] + pv
            m_ref[sel, h] = m_new

    lp = lam_ref[...]
    lam = (jnp.exp(jnp.sum(lp[0:1] * lp[1:2], axis=-1, keepdims=True))
           - jnp.exp(jnp.sum(lp[2:3] * lp[3:4], axis=-1, keepdims=True)) + lambda_init)

    def finish(t):
        for h in range(DA_HEADS):
            o1 = acc_ref[t, h, 0:tq, 0:LANES] / acc_ref[t, h, 0:tq, LANES:2 * LANES]
            o2 = acc_ref[t, h, tq:rows, 0:LANES] / acc_ref[t, h, tq:rows, LANES:2 * LANES]
            o = _rms(o1 - lam * o2, sg_ref[...]) * (1.0 - lambda_init)
            o_ref[t * tq:(t + 1) * tq, h * LANES:(h + 1) * LANES] = o.astype(BF16)

    full_a = pair // 2
    full_block(jnp.where(full_a > 0, 0, 1), 0)
    finish(0)
    for t in range(1, (n_tiles - 2) // 2):
        full_block(1, pl.multiple_of((t - full_a) * width, width))
    finish(1)


def _diff_attention(l, lam_params, q, k, v, subln_g, lambda_init):
    b, s, w = q.shape
    tq = TQ
    n_tiles = s // tq
    assert n_tiles % 4 == 0 and (n_tiles // 2 - 1) // 2 <= 1, "block schedule assumes n_tiles in (4, 8)"
    return pl.pallas_call(
        functools.partial(_attn_kernel, tq=tq, n_tiles=n_tiles, lambda_init=lambda_init),
        grid=(b, n_tiles // 2),
        in_specs=[
            _layer_spec(lam_params, l // 2),
            pl.BlockSpec((None, tq, w), lambda i, j: (i, j, 0)),
            pl.BlockSpec((None, tq, w), lambda i, j: (i, n_tiles - 1 - j, 0)),
            pl.BlockSpec((None, s, w), lambda i, j: (i, 0, 0)),
            pl.BlockSpec((None, s, w), lambda i, j: (i, 0, 0)),
            _layer_spec(_rows(subln_g), l // 2),
        ],
        out_specs=pl.BlockSpec((None, 2 * tq, w), lambda i, j: (i, j, 0)),
        out_shape=jax.ShapeDtypeStruct((b, s, DA_V_WIDTH), BF16),
        scratch_shapes=[
            pltpu.VMEM((2, DA_HEADS, 2 * tq, LANES), BF16),
            pltpu.VMEM((2, DA_HEADS, 2 * tq, LANES), F32),
            pltpu.VMEM((2, DA_HEADS, 2 * tq, 2 * LANES), F32),
            pltpu.VMEM((2, tq, 2 * tq), F32),
        ],
        compiler_params=_params(2),
        name="diff_attn",
    )(lam_params, q, q, k, v, _rows(subln_g))


def _rope_tables(seq):
    dim = DA_HEAD_DIM
    inv = 1.0 / (ROPE_THETA ** (jnp.arange(0, dim, 2, dtype=F32) / dim))
    ang = jnp.arange(seq, dtype=F32)[:, None] * inv[None, :]
    ang = jnp.concatenate([ang, ang], axis=-1)
    cos = jnp.cos(ang)
    sin = jnp.sin(ang)
    sign = jnp.where(jnp.arange(dim) < dim // 2, -1.0, 1.0).astype(F32)
    reps = LANES // dim
    return jnp.tile(cos, (1, reps)), jnp.tile(sin * sign[None, :], (1, reps))


def kernel(x, norm_mix_g, norm_ffn_g, ab_w_in, ab_w_out, diff_lq1, diff_lk1, diff_lq2, diff_lk2,
           diff_subln_g, sg_ln_g, sg_ln_b, sg_w, sg_b, conv_w_in, conv_b_in, conv_dw_w, conv_dw_b,
           conv_ln_g, conv_ln_b, conv_w_out, conv_b_out, ffn_w_up, ffn_dw_w, ffn_dw_b, ffn_w_down,
           final_norm_g):
    seq = x.shape[1]
    cos, sin = _rope_tables(seq)
    ab_w_in = ab_w_in.astype(BF16)
    ab_w_out = ab_w_out.astype(BF16)
    conv_w_in = conv_w_in.astype(BF16)
    conv_w_out = conv_w_out.astype(BF16)
    ffn_w_up = ffn_w_up.astype(BF16)
    ffn_w_down = ffn_w_down.astype(BF16)
    lam_params = jnp.stack([diff_lq1, diff_lk1, diff_lq2, diff_lk2], axis=1)
    ffn_args = (norm_ffn_g, ffn_w_up, ffn_dw_w, ffn_dw_b, ffn_w_down, final_norm_g)
    for l in range(DEPTH):
        if l % 2 == 0:
            lambda_init = 0.8 - 0.6 * math.exp(-0.3 * l)
            q, k, v, u, z = _even_in(x, l, norm_mix_g, ab_w_in, cos, sin, sg_ln_g, sg_ln_b)
            o = _diff_attention(l, lam_params, q, k, v, diff_subln_g, lambda_init)
            x = _even_tail(o, u, z, x, l, sg_w, sg_b, ab_w_out, *ffn_args)
        else:
            x = _odd_layer(x, l, norm_mix_g, conv_w_in, conv_b_in, conv_dw_w, conv_dw_b, conv_ln_g,
                           conv_ln_b, conv_w_out, conv_b_out, *ffn_args)
    return x
```

```python
import functools
import math

import jax
import jax.numpy as jnp
from jax import lax
from jax.experimental import pallas as pl
from jax.experimental.pallas import tpu as pltpu

D_MODEL = 1024
DEPTH = 4
EPS = 1e-6

DA_HEADS = 4
DA_HEAD_DIM = 64
DA_V_DIM = 2 * DA_HEAD_DIM
DA_QK_WIDTH = DA_HEADS * 2 * DA_HEAD_DIM
DA_V_WIDTH = DA_HEADS * DA_V_DIM
ROPE_THETA = 10000.0

SG_GROUPS = 4
SG_CHUNK = 128
SG_GROUP_DIM = 128
SG_WIDTH = SG_GROUPS * SG_GROUP_DIM

AB_IN_WIDTH = 2 * DA_QK_WIDTH + DA_V_WIDTH + 2 * SG_WIDTH
AB_OUT_WIDTH = DA_V_WIDTH + SG_WIDTH

CONV_INNER = D_MODEL
CONV_WIDTH = 31
FFN_HIDDEN = 2816
FFN_CONV_WIDTH = 3

LANES = 128
SUBLANES = 8
VMEM_LIMIT = 56 * 1024 * 1024

F32 = jnp.float32
BF16 = jnp.bfloat16

TQ = 256
TM_IN = 2 * TQ
TM_FFN = 2 * TQ
FFN_CHUNK = 2 * LANES
FFN_SLOTS = 6
ODD_CONV_ROWS = 64
CONV_HALO = 32
FFN_HALO = SUBLANES


def _rms(x, g):
    return x * lax.rsqrt(jnp.mean(x * x, axis=-1, keepdims=True) + EPS) * g


def _params(n_axes):
    return pltpu.CompilerParams(
        dimension_semantics=("arbitrary",) * n_axes, vmem_limit_bytes=VMEM_LIMIT)


def _const_spec(shape):
    nd = len(shape)
    return pl.BlockSpec(shape, lambda *_: (0,) * nd)


def _layer_spec(arr, l, buffers=None):
    tail = arr.shape[1:]
    zeros = (0,) * len(tail)
    kw = {} if buffers is None else {"pipeline_mode": pl.Buffered(buffers)}
    return pl.BlockSpec((None,) + tail, lambda *_: (l,) + zeros, **kw)


def _rows(a):
    return a.reshape(a.shape[0], 1, a.shape[1])


def _ffn_body(x, g, wup_ref, dww_ref, dwb_ref, wdown_ref, hbuf, hhalo, act, tm):
    hid = FFN_HIDDEN
    halo = FFN_HALO
    gate0 = hid // LANES
    per = FFN_CHUNK // LANES
    h = _rms(x, g).astype(BF16)

    def conv(s, p):
        w = dww_ref[:, p * LANES:(p + 1) * LANES]
        y = dwb_ref[:, p * LANES:(p + 1) * LANES] + w[0:1] * hbuf[s, halo - 2:halo - 2 + tm, :]
        y = y + w[1:2] * hbuf[s, halo - 1:halo - 1 + tm, :]
        return y + w[2:3] * hbuf[s, halo:halo + tm, :]

    for ci, c in enumerate(range(0, hid, FFN_CHUNK)):
        base = (ci % FFN_SLOTS) * 2 * per
        for half, c0 in enumerate((c, hid + c)):
            up = jnp.dot(h, wup_ref[:, c0:c0 + FFN_CHUNK], preferred_element_type=F32)
            for i in range(per):
                s, p = base + half * per + i, c0 // LANES + i
                hbuf[s, 0:halo, :] = hhalo[p]
                hbuf[s, halo:halo + tm, :] = up[:, i * LANES:(i + 1) * LANES]
                hhalo[p] = up[tm - halo:tm, i * LANES:(i + 1) * LANES]
        for i in range(per):
            p = c // LANES + i
            a = conv(base + i, p)
            gt = conv(base + per + i, gate0 + p)
            act[:, p * LANES:(p + 1) * LANES] = (a * (gt * jax.nn.sigmoid(gt))).astype(BF16)

    return x + jnp.dot(act[...], wdown_ref[...], preferred_element_type=F32)


def _odd_front(x, g, win_ref, bin_ref, ybuf, tm):
    c_in = CONV_INNER
    halo = CONV_HALO
    h = _rms(x, g).astype(BF16)
    nch = 256
    for c in range(0, c_in, nch):
        a = jnp.dot(h, win_ref[:, c:c + nch], preferred_element_type=F32) + bin_ref[:, c:c + nch]
        gate = (jnp.dot(h, win_ref[:, c_in + c:c_in + c + nch], preferred_element_type=F32)
                + bin_ref[:, c_in + c:c_in + c + nch])
        y = a * jax.nn.sigmoid(gate)
        for i in range(nch // LANES):
            ybuf[c // LANES + i, halo:halo + tm, :] = y[:, i * LANES:(i + 1) * LANES]


def _odd_back(x, dww_ref, dwb_ref, lng_ref, lnb_ref, wout_ref, bout_ref, ybuf, cbuf, tm):
    c_in = CONV_INNER
    halo = CONV_HALO
    rt = ODD_CONV_ROWS
    first = halo - (CONV_WIDTH - 1)
    for p in range(c_in // LANES):
        for r in range(0, tm, rt):
            acc = jnp.broadcast_to(dwb_ref[:, p * LANES:(p + 1) * LANES], (rt, LANES))
            for k in range(CONV_WIDTH):
                acc = acc + dww_ref[k:k + 1, p * LANES:(p + 1) * LANES] * ybuf[p, first + k + r:first + k + r + rt, :]
            cbuf[r:r + rt, p * LANES:(p + 1) * LANES] = acc

    ybuf[:, 0:halo, :] = ybuf[:, tm:tm + halo, :]

    y = cbuf[...]
    mu = jnp.mean(y, axis=-1, keepdims=True)
    yc = y - mu
    y = yc * lax.rsqrt(jnp.mean(yc * yc, axis=-1, keepdims=True) + EPS) * lng_ref[...] + lnb_ref[...]
    y = (y * jax.nn.sigmoid(y)).astype(BF16)
    return x + jnp.dot(y, wout_ref[...], preferred_element_type=F32) + bout_ref[...]


def _even_out_body(oa_ref, ob_ref, u_ref, z_ref, x_ref, sgw_ref, sgb_ref, wout_ref, cat, tm):
    row = lax.broadcasted_iota(jnp.int32, (SG_CHUNK, SG_CHUNK), 0)
    col = lax.broadcasted_iota(jnp.int32, (SG_CHUNK, SG_CHUNK), 1)
    tri = col <= row
    cat[0:TQ, 0:DA_V_WIDTH] = oa_ref[...]
    cat[TQ:tm, 0:DA_V_WIDTH] = ob_ref[...]
    for g in range(SG_GROUPS):
        w = jnp.where(tri, sgw_ref[g], 0.0).astype(BF16)
        bias = sgb_ref[g]
        c0 = g * SG_GROUP_DIM
        for r in range(0, tm, 2 * SG_CHUNK):
            z2 = jnp.concatenate([z_ref[r + i * SG_CHUNK:r + (i + 1) * SG_CHUNK, c0:c0 + SG_GROUP_DIM]
                                  for i in range(2)], axis=1)
            zs2 = jnp.dot(w, z2, preferred_element_type=F32) + bias
            for i in range(2):
                ri = r + i * SG_CHUNK
                gate = u_ref[ri:ri + SG_CHUNK, c0:c0 + SG_GROUP_DIM] * zs2[:, i * SG_GROUP_DIM:(i + 1) * SG_GROUP_DIM]
                cat[ri:ri + SG_CHUNK, DA_V_WIDTH + c0:DA_V_WIDTH + c0 + SG_GROUP_DIM] = gate.astype(BF16)
    return x_ref[...] + jnp.dot(cat[...], wout_ref[...], preferred_element_type=F32)


N_MIXER_REFS = {"odd": 10, "even": 8}


def _tail_kernel(*refs, mode, tm, n_tiles, n_steps, final):
    n_mix = N_MIXER_REFS[mode]
    mix_refs = refs[:n_mix]
    fg_ref, wup_ref, fdw_ref, fdb_ref, wdown_ref, fing_ref, out_ref = refs[n_mix:n_mix + 7]
    x1buf, hbuf, hhalo, act = refs[n_mix + 7:n_mix + 11]
    mix_scratch = refs[n_mix + 11:]
    g = pl.program_id(0)

    @pl.when(g > 0)
    def _():
        @pl.when((g - 1) % n_tiles == 0)
        def _():
            hhalo[...] = jnp.zeros(hhalo.shape, F32)

        y = _ffn_body(x1buf[(g + 1) % 2], fg_ref[...], wup_ref, fdw_ref, fdb_ref, wdown_ref, hbuf, hhalo,
                      act, tm)
        if final:
            y = _rms(y, fing_ref[...])
        out_ref[...] = y

    @pl.when(g < n_steps)
    def _():
        if mode == "odd":
            x_ref, mg_ref, win_ref, bin_ref = mix_refs[:4]
            ybuf, cbuf = mix_scratch

            @pl.when(g % n_tiles == 0)
            def _():
                ybuf[:, 0:CONV_HALO, :] = jnp.zeros((ybuf.shape[0], CONV_HALO, LANES), F32)

            _odd_front(x_ref[...], mg_ref[...], win_ref, bin_ref, ybuf, tm)
            x1 = _odd_back(x_ref[...], *mix_refs[4:], ybuf, cbuf, tm)
        else:
            x1 = _even_out_body(*mix_refs, *mix_scratch, tm)
        x1buf[g % 2] = x1


def _layer_tail(mode, l, mix_args, mix_specs, mix_scratch, shape, norm_ffn_g, ffn_w_up, ffn_dw_w,
                ffn_dw_b, ffn_w_down, final_norm_g):
    b, s, d = shape
    tm = TM_FFN
    hid = FFN_HIDDEN
    n_tiles = s // tm
    n_steps = b * n_tiles
    tile_b = lambda g: jnp.maximum(g - 1, 0)
    return pl.pallas_call(
        functools.partial(_tail_kernel, mode=mode, tm=tm, n_tiles=n_tiles, n_steps=n_steps,
                          final=(l == DEPTH - 1)),
        grid=(n_steps + 1,),
        in_specs=mix_specs + [
            _layer_spec(_rows(norm_ffn_g), l),
            _layer_spec(ffn_w_up, l, buffers=1),
            _layer_spec(ffn_dw_w, l),
            _layer_spec(_rows(ffn_dw_b), l),
            _layer_spec(ffn_w_down, l, buffers=1),
            _const_spec((1, d)),
        ],
        out_specs=pl.BlockSpec((None, tm, d), lambda g: (tile_b(g) // n_tiles, tile_b(g) % n_tiles, 0)),
        out_shape=jax.ShapeDtypeStruct((b, s, d), F32),
        scratch_shapes=[
            pltpu.VMEM((2, tm, d), F32),
            pltpu.VMEM((FFN_SLOTS * 2 * FFN_CHUNK // LANES, tm + FFN_HALO, LANES), F32),
            pltpu.VMEM((2 * hid // LANES, FFN_HALO, LANES), F32),
            pltpu.VMEM((tm, hid), BF16),
        ] + mix_scratch,
        compiler_params=_params(1),
        name=mode + "_tail",
    )(*mix_args, _rows(norm_ffn_g), ffn_w_up, ffn_dw_w, _rows(ffn_dw_b), ffn_w_down,
      final_norm_g.reshape(1, d))


def _tile_a_spec(width, n_tiles, n_steps):
    def index(g):
        t = jnp.minimum(g, n_steps - 1)
        return (t // n_tiles, t % n_tiles, 0)
    return pl.BlockSpec((None, TM_FFN, width), index)


def _attn_tile_spec(k, n_tiles, n_steps):
    per = TM_FFN // TQ

    def index(g):
        t = jnp.minimum(g, n_steps - 1)
        return (t // n_tiles, _pair_perm((t % n_tiles) * per + k, n_tiles * per), 0)
    return pl.BlockSpec((None, TQ, DA_V_WIDTH), index)


def _odd_layer(x, l, norm_mix_g, conv_w_in, conv_b_in, conv_dw_w, conv_dw_b, conv_ln_g, conv_ln_b,
               conv_w_out, conv_b_out, *ffn_args):
    b, s, d = x.shape
    i = l // 2
    c_in = CONV_INNER
    n_tiles = s // TM_FFN
    specs = [
        _tile_a_spec(d, n_tiles, b * n_tiles),
        _layer_spec(_rows(norm_mix_g), l),
        _layer_spec(conv_w_in, i, buffers=1),
        _layer_spec(_rows(conv_b_in), i),
        _layer_spec(conv_dw_w, i),
        _layer_spec(_rows(conv_dw_b), i),
        _layer_spec(_rows(conv_ln_g), i),
        _layer_spec(_rows(conv_ln_b), i),
        _layer_spec(conv_w_out, i, buffers=1),
        _layer_spec(_rows(conv_b_out), i),
    ]
    args = [x, _rows(norm_mix_g), conv_w_in, _rows(conv_b_in), conv_dw_w, _rows(conv_dw_b),
            _rows(conv_ln_g), _rows(conv_ln_b), conv_w_out, _rows(conv_b_out)]
    scratch = [
        pltpu.VMEM((c_in // LANES, TM_FFN + CONV_HALO, LANES), F32),
        pltpu.VMEM((TM_FFN, c_in), F32),
    ]
    return _layer_tail("odd", l, args, specs, scratch, x.shape, *ffn_args)


def _even_tail(o, u, z, x, l, sg_w, sg_b, ab_w_out, *ffn_args):
    b, s, d = x.shape
    i = l // 2
    n_tiles = s // TM_FFN
    n_steps = b * n_tiles
    sg_b = sg_b.reshape(sg_b.shape + (1,))
    assert TM_FFN == 2 * TQ
    specs = [
        _attn_tile_spec(0, n_tiles, n_steps),
        _attn_tile_spec(1, n_tiles, n_steps),
        _tile_a_spec(SG_WIDTH, n_tiles, n_steps),
        _tile_a_spec(SG_WIDTH, n_tiles, n_steps),
        _tile_a_spec(d, n_tiles, n_steps),
        _layer_spec(sg_w, i),
        _layer_spec(sg_b, i),
        _layer_spec(ab_w_out, i, buffers=1),
    ]
    scratch = [pltpu.VMEM((TM_FFN, AB_OUT_WIDTH), BF16)]
    return _layer_tail("even", l, [o, o, u, z, x, sg_w, sg_b, ab_w_out], specs, scratch, x.shape, *ffn_args)


def _gelu_tanh(x):
    return 0.5 * x * (1.0 + jnp.tanh(math.sqrt(2.0 / math.pi) * (x + 0.044715 * (x * x * x))))


def _even_in_kernel(x_ref, g_ref, w_ref, cos_ref, sin_ref, lng_ref, lnb_ref,
                    q_ref, k_ref, v_ref, u_ref, z_ref, *, tm):
    x = x_ref[...]
    h = _rms(x, g_ref[...]).astype(BF16)
    cos = cos_ref[...]
    sin = sin_ref[...]
    lane = lax.broadcasted_iota(jnp.int32, (tm, LANES), 1)
    lower = (lane % DA_HEAD_DIM) < (DA_HEAD_DIM // 2)
    half = DA_HEAD_DIM // 2

    def rope_store(dst_ref, col0, scale):
        t_all = jnp.dot(h, w_ref[:, col0:col0 + DA_QK_WIDTH], preferred_element_type=F32)
        cos_s = cos * scale
        sin_s = sin * scale
        for j in range(0, DA_QK_WIDTH, LANES):
            t = t_all[:, j:j + LANES]
            rot = jnp.where(lower, pltpu.roll(t, LANES - half, axis=1), pltpu.roll(t, half, axis=1))
            dst_ref[:, j:j + LANES] = (t * cos_s + rot * sin_s).astype(BF16)

    o1 = 2 * DA_QK_WIDTH
    o2 = o1 + DA_V_WIDTH
    o3 = o2 + SG_WIDTH
    z_all = _gelu_tanh(jnp.dot(h, w_ref[:, o3:o3 + SG_WIDTH], preferred_element_type=F32))
    for j in range(0, SG_WIDTH, SG_GROUP_DIM):
        z = z_all[:, j:j + SG_GROUP_DIM]
        mu = jnp.mean(z, axis=-1, keepdims=True)
        zc = z - mu
        z = zc * lax.rsqrt(jnp.mean(zc * zc, axis=-1, keepdims=True) + EPS)
        z = z * lng_ref[:, j:j + SG_GROUP_DIM] + lnb_ref[:, j:j + SG_GROUP_DIM]
        z_ref[:, j:j + SG_GROUP_DIM] = z.astype(BF16)
    u_ref[...] = _gelu_tanh(jnp.dot(h, w_ref[:, o2:o2 + SG_WIDTH], preferred_element_type=F32))
    rope_store(q_ref, 0, DA_HEAD_DIM ** -0.5 * math.log2(math.e))
    rope_store(k_ref, DA_QK_WIDTH, 1.0)
    v_ref[...] = jnp.dot(h, w_ref[:, o1:o1 + DA_V_WIDTH], preferred_element_type=F32).astype(BF16)


def _even_in(x, l, norm_mix_g, ab_w_in, cos, sin, sg_ln_g, sg_ln_b):
    b, s, d = x.shape
    i = l // 2
    tm = TM_IN
    n_tiles = s // tm
    row_spec = lambda w: pl.BlockSpec((None, tm, w), lambda i, j: (i, j, 0))
    return pl.pallas_call(
        functools.partial(_even_in_kernel, tm=tm),
        grid=(b, n_tiles),
        in_specs=[
            row_spec(d),
            _layer_spec(_rows(norm_mix_g), l),
            _layer_spec(ab_w_in, i),
            pl.BlockSpec((tm, LANES), lambda i, j: (j, 0)),
            pl.BlockSpec((tm, LANES), lambda i, j: (j, 0)),
            _layer_spec(_rows(sg_ln_g), i),
            _layer_spec(_rows(sg_ln_b), i),
        ],
        out_specs=[row_spec(DA_QK_WIDTH), row_spec(DA_QK_WIDTH), row_spec(DA_V_WIDTH),
                   row_spec(SG_WIDTH), row_spec(SG_WIDTH)],
        out_shape=[
            jax.ShapeDtypeStruct((b, s, DA_QK_WIDTH), BF16),
            jax.ShapeDtypeStruct((b, s, DA_QK_WIDTH), BF16),
            jax.ShapeDtypeStruct((b, s, DA_V_WIDTH), BF16),
            jax.ShapeDtypeStruct((b, s, SG_WIDTH), F32),
            jax.ShapeDtypeStruct((b, s, SG_WIDTH), BF16),
        ],
        compiler_params=_params(2),
        name="even_in",
    )(x, _rows(norm_mix_g), ab_w_in, cos, sin, _rows(sg_ln_g), _rows(sg_ln_b))


def _pair_perm(j, n_tiles):
    return jnp.where(j < n_tiles // 2, 2 * j, 2 * (n_tiles - 1 - j) + 1)


def _attn_kernel(lam_ref, qa_ref, qb_ref, k_ref, v_ref, sg_ref, o_ref, qs_ref, m_ref, acc_ref, bias_ref,
                 *, tq, n_tiles, lambda_init):
    pair = pl.program_id(1)
    rows = 2 * tq
    width = 2 * tq
    last = n_tiles - 1
    nt = (((1,), (1,)), ((), ()))
    lane = lax.broadcasted_iota(jnp.int32, (tq, LANES), 1)
    ones = jnp.ones((width, LANES), BF16)

    def scores(sel, h, start):
        k = k_ref[pl.ds(start, width), h * LANES:(h + 1) * LANES]
        return lax.dot_general(qs_ref[sel, h], k, nt, preferred_element_type=F32)

    def values(h, start):
        return jnp.concatenate([v_ref[pl.ds(start, width), h * LANES:(h + 1) * LANES], ones], axis=1)

    def spread(m):
        return jnp.concatenate([m] * (width // LANES), axis=1)

    for t, tile in enumerate((pair, last - pair)):
        blk = tile // 2
        start = pl.multiple_of(blk * width, width)
        r = lax.broadcasted_iota(jnp.int32, (tq, width), 0)
        c = lax.broadcasted_iota(jnp.int32, (tq, width), 1)
        bias = jnp.where(c <= r + (tile * tq - blk * width), 0.0, -jnp.inf).astype(F32)
        bias_ref[t] = bias
        for h in range(DA_HEADS):
            qh = (qa_ref, qb_ref)[t][:, h * LANES:(h + 1) * LANES]
            zero = jnp.zeros_like(qh)
            qs_ref[t, h, 0:tq, :] = jnp.where(lane < DA_HEAD_DIM, qh, zero)
            qs_ref[t, h, tq:rows, :] = jnp.where(lane >= DA_HEAD_DIM, qh, zero)
            s = scores(t, h, start) + jnp.concatenate([bias_ref[t]] * 2, axis=0)
            m = jnp.broadcast_to(jnp.max(s, axis=-1, keepdims=True), (rows, LANES))
            p = jnp.exp2(s - spread(m)).astype(BF16)
            acc_ref[t, h] = jnp.dot(p, values(h, start), preferred_element_type=F32)
            m_ref[t, h] = m

    def full_block(sel, start):
        for h in range(DA_HEADS):
            s = scores(sel, h, start)
            m_old = m_ref[sel, h]
            m_new = jnp.maximum(m_old, jnp.max(s, axis=-1, keepdims=True))
            alpha = jnp.exp2(m_old - m_new)
            p = jnp.exp2(s - spread(m_new)).astype(BF16)
            pv = jnp.dot(p, values(h, start), preferred_element_type=F32)
            acc_ref[sel, h] = jnp.concatenate([alpha, alpha], axis=1) * acc_ref[sel, h] + pv
            m_ref[sel, h] = m_new

    lp = lam_ref[...]
    lam = (jnp.exp(jnp.sum(lp[0:1] * lp[1:2], axis=-1, keepdims=True))
           - jnp.exp(jnp.sum(lp[2:3] * lp[3:4], axis=-1, keepdims=True)) + lambda_init)

    def finish(t):
        for h in range(DA_HEADS):
            o1 = acc_ref[t, h, 0:tq, 0:LANES] / acc_ref[t, h, 0:tq, LANES:2 * LANES]
            o2 = acc_ref[t, h, tq:rows, 0:LANES] / acc_ref[t, h, tq:rows, LANES:2 * LANES]
            o = _rms(o1 - lam * o2, sg_ref[...]) * (1.0 - lambda_init)
            o_ref[t * tq:(t + 1) * tq, h * LANES:(h + 1) * LANES] = o.astype(BF16)

    full_a = pair // 2
    full_block(jnp.where(full_a > 0, 0, 1), 0)
    finish(0)
    for t in range(1, (n_tiles - 2) // 2):
        full_block(1, pl.multiple_of((t - full_a) * width, width))
    finish(1)


def _diff_attention(l, lam_params, q, k, v, subln_g, lambda_init):
    b, s, w = q.shape
    tq = TQ
    n_tiles = s // tq
    assert n_tiles % 4 == 0 and (n_tiles // 2 - 1) // 2 <= 1, "block schedule assumes n_tiles in (4, 8)"
    return pl.pallas_call(
        functools.partial(_attn_kernel, tq=tq, n_tiles=n_tiles, lambda_init=lambda_init),
        grid=(b, n_tiles // 2),
        in_specs=[
            _layer_spec(lam_params, l // 2),
            pl.BlockSpec((None, tq, w), lambda i, j: (i, j, 0)),
            pl.BlockSpec((None, tq, w), lambda i, j: (i, n_tiles - 1 - j, 0)),
            pl.BlockSpec((None, s, w), lambda i, j: (i, 0, 0)),
            pl.BlockSpec((None, s, w), lambda i, j: (i, 0, 0)),
            _layer_spec(_rows(subln_g), l // 2),
        ],
        out_specs=pl.BlockSpec((None, 2 * tq, w), lambda i, j: (i, j, 0)),
        out_shape=jax.ShapeDtypeStruct((b, s, DA_V_WIDTH), BF16),
        scratch_shapes=[
            pltpu.VMEM((2, DA_HEADS, 2 * tq, LANES), BF16),
            pltpu.VMEM((2, DA_HEADS, 2 * tq, LANES), F32),
            pltpu.VMEM((2, DA_HEADS, 2 * tq, 2 * LANES), F32),
            pltpu.VMEM((2, tq, 2 * tq), F32),
        ],
        compiler_params=_params(2),
        name="diff_attn",
    )(lam_params, q, q, k, v, _rows(subln_g))


def _rope_tables(seq):
    dim = DA_HEAD_DIM
    inv = 1.0 / (ROPE_THETA ** (jnp.arange(0, dim, 2, dtype=F32) / dim))
    ang = jnp.arange(seq, dtype=F32)[:, None] * inv[None, :]
    ang = jnp.concatenate([ang, ang], axis=-1)
    cos = jnp.cos(ang)
    sin = jnp.sin(ang)
    sign = jnp.where(jnp.arange(dim) < dim // 2, -1.0, 1.0).astype(F32)
    reps = LANES // dim
    return jnp.tile(cos, (1, reps)), jnp.tile(sin * sign[None, :], (1, reps))


def kernel(x, norm_mix_g, norm_ffn_g, ab_w_in, ab_w_out, diff_lq1, diff_lk1, diff_lq2, diff_lk2,
           diff_subln_g, sg_ln_g, sg_ln_b, sg_w, sg_b, conv_w_in, conv_b_in, conv_dw_w, conv_dw_b,
           conv_ln_g, conv_ln_b, conv_w_out, conv_b_out, ffn_w_up, ffn_dw_w, ffn_dw_b, ffn_w_down,
           final_norm_g):
    seq = x.shape[1]
    cos, sin = _rope_tables(seq)
    ab_w_in = ab_w_in.astype(BF16)
    ab_w_out = ab_w_out.astype(BF16)
    conv_w_in = conv_w_in.astype(BF16)
    conv_w_out = conv_w_out.astype(BF16)
    ffn_w_up = ffn_w_up.astype(BF16)
    ffn_w_down = ffn_w_down.astype(BF16)
    lam_params = jnp.stack([diff_lq1, diff_lk1, diff_lq2, diff_lk2], axis=1)
    ffn_args = (norm_ffn_g, ffn_w_up, ffn_dw_w, ffn_dw_b, ffn_w_down, final_norm_g)
    for l in range(DEPTH):
        if l % 2 == 0:
            lambda_init = 0.8 - 0.6 * math.exp(-0.3 * l)
            q, k, v, u, z = _even_in(x, l, norm_mix_g, ab_w_in, cos, sin, sg_ln_g, sg_ln_b)
            o = _diff_attention(l, lam_params, q, k, v, diff_subln_g, lambda_init)
            x = _even_tail(o, u, z, x, l, sg_w, sg_b, ab_w_out, *ffn_args)
        else:
            x = _odd_layer(x, l, norm_mix_g, conv_w_in, conv_b_in, conv_dw_w, conv_dw_b, conv_ln_g,
                           conv_ln_b, conv_w_out, conv_b_out, *ffn_args)
    return x
```

```python
import functools
import math

import jax
import jax.numpy as jnp
from jax import lax
from jax.experimental import pallas as pl
from jax.experimental.pallas import tpu as pltpu

D_MODEL = 1024
DEPTH = 4
EPS = 1e-6

DA_HEADS = 4
DA_HEAD_DIM = 64
DA_V_DIM = 2 * DA_HEAD_DIM
DA_QK_WIDTH = DA_HEADS * 2 * DA_HEAD_DIM
DA_V_WIDTH = DA_HEADS * DA_V_DIM
ROPE_THETA = 10000.0

SG_GROUPS = 4
SG_CHUNK = 128
SG_GROUP_DIM = 128
SG_WIDTH = SG_GROUPS * SG_GROUP_DIM

AB_IN_WIDTH = 2 * DA_QK_WIDTH + DA_V_WIDTH + 2 * SG_WIDTH
AB_OUT_WIDTH = DA_V_WIDTH + SG_WIDTH

CONV_INNER = D_MODEL
CONV_WIDTH = 31
FFN_HIDDEN = 2816
FFN_CONV_WIDTH = 3

LANES = 128
SUBLANES = 8
VMEM_LIMIT = 56 * 1024 * 1024

F32 = jnp.float32
BF16 = jnp.bfloat16

TQ = 256
ATTN_PAIRS = 2
TM_IN = 2 * TQ
TM_FFN = 2 * TQ
FFN_CHUNK = 2 * LANES
FFN_SLOTS = 6
ODD_CONV_ROWS = 128
CONV_HALO = 32
FFN_HALO = SUBLANES


def _rms(x, g):
    return x * lax.rsqrt(jnp.mean(x * x, axis=-1, keepdims=True) + EPS) * g


def _params(n_axes):
    return pltpu.CompilerParams(
        dimension_semantics=("arbitrary",) * n_axes, vmem_limit_bytes=VMEM_LIMIT)


def _const_spec(shape):
    nd = len(shape)
    return pl.BlockSpec(shape, lambda *_: (0,) * nd)


def _layer_spec(arr, l, buffers=None):
    tail = arr.shape[1:]
    zeros = (0,) * len(tail)
    kw = {} if buffers is None else {"pipeline_mode": pl.Buffered(buffers)}
    return pl.BlockSpec((None,) + tail, lambda *_: (l,) + zeros, **kw)


def _rows(a):
    return a.reshape(a.shape[0], 1, a.shape[1])


def _ffn_body(x, g, wup_ref, dww_ref, dwb_ref, wdown_ref, hbuf, hhalo, act, tm):
    hid = FFN_HIDDEN
    halo = FFN_HALO
    gate0 = hid // LANES
    per = FFN_CHUNK // LANES
    h = _rms(x, g).astype(BF16)

    def conv(s, p):
        w = dww_ref[:, p * LANES:(p + 1) * LANES]
        y = dwb_ref[:, p * LANES:(p + 1) * LANES] + w[0:1] * hbuf[s, halo - 2:halo - 2 + tm, :]
        y = y + w[1:2] * hbuf[s, halo - 1:halo - 1 + tm, :]
        return y + w[2:3] * hbuf[s, halo:halo + tm, :]

    for ci, c in enumerate(range(0, hid, FFN_CHUNK)):
        base = (ci % FFN_SLOTS) * 2 * per
        for half, c0 in enumerate((c, hid + c)):
            up = jnp.dot(h, wup_ref[:, c0:c0 + FFN_CHUNK], preferred_element_type=F32)
            for i in range(per):
                s, p = base + half * per + i, c0 // LANES + i
                hbuf[s, 0:halo, :] = hhalo[p]
                hbuf[s, halo:halo + tm, :] = up[:, i * LANES:(i + 1) * LANES]
                hhalo[p] = up[tm - halo:tm, i * LANES:(i + 1) * LANES]
        for i in range(per):
            p = c // LANES + i
            a = conv(base + i, p)
            gt = conv(base + per + i, gate0 + p)
            act[:, p * LANES:(p + 1) * LANES] = (a * (gt * jax.nn.sigmoid(gt))).astype(BF16)

    return x + jnp.dot(act[...], wdown_ref[...], preferred_element_type=F32)


def _odd_front(x, g, win_ref, bin_ref, ybuf, tm):
    c_in = CONV_INNER
    halo = CONV_HALO
    h = _rms(x, g).astype(BF16)
    nch = 256
    for c in range(0, c_in, nch):
        a = jnp.dot(h, win_ref[:, c:c + nch], preferred_element_type=F32) + bin_ref[:, c:c + nch]
        gate = (jnp.dot(h, win_ref[:, c_in + c:c_in + c + nch], preferred_element_type=F32)
                + bin_ref[:, c_in + c:c_in + c + nch])
        y = a * jax.nn.sigmoid(gate)
        for i in range(nch // LANES):
            ybuf[c // LANES + i, halo:halo + tm, :] = y[:, i * LANES:(i + 1) * LANES]


def _odd_back(x, dww_ref, dwb_ref, lng_ref, lnb_ref, wout_ref, bout_ref, ybuf, cbuf, tm):
    c_in = CONV_INNER
    halo = CONV_HALO
    rt = ODD_CONV_ROWS
    first = halo - (CONV_WIDTH - 1)
    for p in range(c_in // LANES):
        for r in range(0, tm, rt):
            acc = jnp.broadcast_to(dwb_ref[:, p * LANES:(p + 1) * LANES], (rt, LANES))
            for k in range(CONV_WIDTH):
                acc = acc + dww_ref[k:k + 1, p * LANES:(p + 1) * LANES] * ybuf[p, first + k + r:first + k + r + rt, :]
            cbuf[r:r + rt, p * LANES:(p + 1) * LANES] = acc

    ybuf[:, 0:halo, :] = ybuf[:, tm:tm + halo, :]

    y = cbuf[...]
    mu = jnp.mean(y, axis=-1, keepdims=True)
    yc = y - mu
    y = yc * lax.rsqrt(jnp.mean(yc * yc, axis=-1, keepdims=True) + EPS) * lng_ref[...] + lnb_ref[...]
    y = (y * jax.nn.sigmoid(y)).astype(BF16)
    return x + jnp.dot(y, wout_ref[...], preferred_element_type=F32) + bout_ref[...]


def _even_out_body(oa_ref, ob_ref, u_ref, z_ref, x_ref, sgw_ref, sgb_ref, wout_ref, cat, tm):
    row = lax.broadcasted_iota(jnp.int32, (SG_CHUNK, SG_CHUNK), 0)
    col = lax.broadcasted_iota(jnp.int32, (SG_CHUNK, SG_CHUNK), 1)
    tri = col <= row
    cat[0:TQ, 0:DA_V_WIDTH] = oa_ref[...]
    cat[TQ:tm, 0:DA_V_WIDTH] = ob_ref[...]
    for g in range(SG_GROUPS):
        w = jnp.where(tri, sgw_ref[g], 0.0).astype(BF16)
        bias = sgb_ref[g]
        c0 = g * SG_GROUP_DIM
        for r in range(0, tm, 2 * SG_CHUNK):
            z2 = jnp.concatenate([z_ref[r + i * SG_CHUNK:r + (i + 1) * SG_CHUNK, c0:c0 + SG_GROUP_DIM]
                                  for i in range(2)], axis=1)
            zs2 = jnp.dot(w, z2, preferred_element_type=F32) + bias
            for i in range(2):
                ri = r + i * SG_CHUNK
                gate = u_ref[ri:ri + SG_CHUNK, c0:c0 + SG_GROUP_DIM] * zs2[:, i * SG_GROUP_DIM:(i + 1) * SG_GROUP_DIM]
                cat[ri:ri + SG_CHUNK, DA_V_WIDTH + c0:DA_V_WIDTH + c0 + SG_GROUP_DIM] = gate.astype(BF16)
    return x_ref[...] + jnp.dot(cat[...], wout_ref[...], preferred_element_type=F32)


N_MIXER_REFS = {"odd": 10, "even": 8}


def _tail_kernel(*refs, mode, tm, n_tiles, n_steps, final):
    n_mix = N_MIXER_REFS[mode]
    mix_refs = refs[:n_mix]
    fg_ref, wup_ref, fdw_ref, fdb_ref, wdown_ref, fing_ref, out_ref = refs[n_mix:n_mix + 7]
    x1buf, hbuf, hhalo, act = refs[n_mix + 7:n_mix + 11]
    mix_scratch = refs[n_mix + 11:]
    g = pl.program_id(0)

    @pl.when(g > 0)
    def _():
        @pl.when((g - 1) % n_tiles == 0)
        def _():
            hhalo[...] = jnp.zeros(hhalo.shape, F32)

        y = _ffn_body(x1buf[(g + 1) % 2], fg_ref[...], wup_ref, fdw_ref, fdb_ref, wdown_ref, hbuf, hhalo,
                      act, tm)
        if final:
            y = _rms(y, fing_ref[...])
        out_ref[...] = y

    @pl.when(g < n_steps)
    def _():
        if mode == "odd":
            x_ref, mg_ref, win_ref, bin_ref = mix_refs[:4]
            ybuf, cbuf = mix_scratch

            @pl.when(g % n_tiles == 0)
            def _():
                ybuf[:, 0:CONV_HALO, :] = jnp.zeros((ybuf.shape[0], CONV_HALO, LANES), F32)

            _odd_front(x_ref[...], mg_ref[...], win_ref, bin_ref, ybuf, tm)
            x1 = _odd_back(x_ref[...], *mix_refs[4:], ybuf, cbuf, tm)
        else:
            x1 = _even_out_body(*mix_refs, *mix_scratch, tm)
        x1buf[g % 2] = x1


def _layer_tail(mode, l, mix_args, mix_specs, mix_scratch, shape, norm_ffn_g, ffn_w_up, ffn_dw_w,
                ffn_dw_b, ffn_w_down, final_norm_g):
    b, s, d = shape
    tm = TM_FFN
    hid = FFN_HIDDEN
    n_tiles = s // tm
    n_steps = b * n_tiles
    tile_b = lambda g: jnp.maximum(g - 1, 0)
    return pl.pallas_call(
        functools.partial(_tail_kernel, mode=mode, tm=tm, n_tiles=n_tiles, n_steps=n_steps,
                          final=(l == DEPTH - 1)),
        grid=(n_steps + 1,),
        in_specs=mix_specs + [
            _layer_spec(_rows(norm_ffn_g), l),
            _layer_spec(ffn_w_up, l, buffers=1),
            _layer_spec(ffn_dw_w, l),
            _layer_spec(_rows(ffn_dw_b), l),
            _layer_spec(ffn_w_down, l, buffers=1),
            _const_spec((1, d)),
        ],
        out_specs=pl.BlockSpec((None, tm, d), lambda g: (tile_b(g) // n_tiles, tile_b(g) % n_tiles, 0)),
        out_shape=jax.ShapeDtypeStruct((b, s, d), F32),
        scratch_shapes=[
            pltpu.VMEM((2, tm, d), F32),
            pltpu.VMEM((FFN_SLOTS * 2 * FFN_CHUNK // LANES, tm + FFN_HALO, LANES), F32),
            pltpu.VMEM((2 * hid // LANES, FFN_HALO, LANES), F32),
            pltpu.VMEM((tm, hid), BF16),
        ] + mix_scratch,
        compiler_params=_params(1),
        name=mode + "_tail",
    )(*mix_args, _rows(norm_ffn_g), ffn_w_up, ffn_dw_w, _rows(ffn_dw_b), ffn_w_down,
      final_norm_g.reshape(1, d))


def _tile_a_spec(width, n_tiles, n_steps):
    def index(g):
        t = jnp.minimum(g, n_steps - 1)
        return (t // n_tiles, t % n_tiles, 0)
    return pl.BlockSpec((None, TM_FFN, width), index)


def _attn_tile_spec(k, n_tiles, n_steps):
    per = TM_FFN // TQ

    def index(g):
        t = jnp.minimum(g, n_steps - 1)
        return (t // n_tiles, _pair_perm((t % n_tiles) * per + k, n_tiles * per), 0)
    return pl.BlockSpec((None, TQ, DA_V_WIDTH), index)


def _odd_layer(x, l, norm_mix_g, conv_w_in, conv_b_in, conv_dw_w, conv_dw_b, conv_ln_g, conv_ln_b,
               conv_w_out, conv_b_out, *ffn_args):
    b, s, d = x.shape
    i = l // 2
    c_in = CONV_INNER
    n_tiles = s // TM_FFN
    specs = [
        _tile_a_spec(d, n_tiles, b * n_tiles),
        _layer_spec(_rows(norm_mix_g), l),
        _layer_spec(conv_w_in, i, buffers=1),
        _layer_spec(_rows(conv_b_in), i),
        _layer_spec(conv_dw_w, i),
        _layer_spec(_rows(conv_dw_b), i),
        _layer_spec(_rows(conv_ln_g), i),
        _layer_spec(_rows(conv_ln_b), i),
        _layer_spec(conv_w_out, i, buffers=1),
        _layer_spec(_rows(conv_b_out), i),
    ]
    args = [x, _rows(norm_mix_g), conv_w_in, _rows(conv_b_in), conv_dw_w, _rows(conv_dw_b),
            _rows(conv_ln_g), _rows(conv_ln_b), conv_w_out, _rows(conv_b_out)]
    scratch = [
        pltpu.VMEM((c_in // LANES, TM_FFN + CONV_HALO, LANES), F32),
        pltpu.VMEM((TM_FFN, c_in), F32),
    ]
    return _layer_tail("odd", l, args, specs, scratch, x.shape, *ffn_args)


def _even_tail(o, u, z, x, l, sg_w, sg_b, ab_w_out, *ffn_args):
    b, s, d = x.shape
    i = l // 2
    n_tiles = s // TM_FFN
    n_steps = b * n_tiles
    sg_b = sg_b.reshape(sg_b.shape + (1,))
    assert TM_FFN == 2 * TQ
    specs = [
        _attn_tile_spec(0, n_tiles, n_steps),
        _attn_tile_spec(1, n_tiles, n_steps),
        _tile_a_spec(SG_WIDTH, n_tiles, n_steps),
        _tile_a_spec(SG_WIDTH, n_tiles, n_steps),
        _tile_a_spec(d, n_tiles, n_steps),
        _layer_spec(sg_w, i),
        _layer_spec(sg_b, i),
        _layer_spec(ab_w_out, i, buffers=1),
    ]
    scratch = [pltpu.VMEM((TM_FFN, AB_OUT_WIDTH), BF16)]
    return _layer_tail("even", l, [o, o, u, z, x, sg_w, sg_b, ab_w_out], specs, scratch, x.shape, *ffn_args)


def _gelu_tanh(x):
    return 0.5 * x * (1.0 + jnp.tanh(math.sqrt(2.0 / math.pi) * (x + 0.044715 * (x * x * x))))


def _even_in_kernel(x_ref, g_ref, w_ref, cos_ref, sin_ref, lng_ref, lnb_ref,
                    q_ref, k_ref, v_ref, u_ref, z_ref, *, tm):
    x = x_ref[...]
    h = _rms(x, g_ref[...]).astype(BF16)
    cos = cos_ref[...]
    sin = sin_ref[...]
    lane = lax.broadcasted_iota(jnp.int32, (tm, LANES), 1)
    lower = (lane % DA_HEAD_DIM) < (DA_HEAD_DIM // 2)
    half = DA_HEAD_DIM // 2

    def rope_store(dst_ref, col0, scale):
        t_all = jnp.dot(h, w_ref[:, col0:col0 + DA_QK_WIDTH], preferred_element_type=F32)
        cos_s = cos * scale
        sin_s = sin * scale
        for j in range(0, DA_QK_WIDTH, LANES):
            t = t_all[:, j:j + LANES]
            rot = jnp.where(lower, pltpu.roll(t, LANES - half, axis=1), pltpu.roll(t, half, axis=1))
            dst_ref[:, j:j + LANES] = (t * cos_s + rot * sin_s).astype(BF16)

    o1 = 2 * DA_QK_WIDTH
    o2 = o1 + DA_V_WIDTH
    o3 = o2 + SG_WIDTH
    z_all = _gelu_tanh(jnp.dot(h, w_ref[:, o3:o3 + SG_WIDTH], preferred_element_type=F32))
    for j in range(0, SG_WIDTH, SG_GROUP_DIM):
        z = z_all[:, j:j + SG_GROUP_DIM]
        mu = jnp.mean(z, axis=-1, keepdims=True)
        zc = z - mu
        z = zc * lax.rsqrt(jnp.mean(zc * zc, axis=-1, keepdims=True) + EPS)
        z = z * lng_ref[:, j:j + SG_GROUP_DIM] + lnb_ref[:, j:j + SG_GROUP_DIM]
        z_ref[:, j:j + SG_GROUP_DIM] = z.astype(BF16)
    u_ref[...] = _gelu_tanh(jnp.dot(h, w_ref[:, o2:o2 + SG_WIDTH], preferred_element_type=F32))
    rope_store(q_ref, 0, DA_HEAD_DIM ** -0.5 * math.log2(math.e))
    rope_store(k_ref, DA_QK_WIDTH, 1.0)
    v_ref[...] = jnp.dot(h, w_ref[:, o1:o1 + DA_V_WIDTH], preferred_element_type=F32).astype(BF16)


def _even_in(x, l, norm_mix_g, ab_w_in, cos, sin, sg_ln_g, sg_ln_b):
    b, s, d = x.shape
    i = l // 2
    tm = TM_IN
    n_tiles = s // tm
    row_spec = lambda w: pl.BlockSpec((None, tm, w), lambda i, j: (i, j, 0))
    return pl.pallas_call(
        functools.partial(_even_in_kernel, tm=tm),
        grid=(b, n_tiles),
        in_specs=[
            row_spec(d),
            _layer_spec(_rows(norm_mix_g), l),
            _layer_spec(ab_w_in, i),
            pl.BlockSpec((tm, LANES), lambda i, j: (j, 0)),
            pl.BlockSpec((tm, LANES), lambda i, j: (j, 0)),
            _layer_spec(_rows(sg_ln_g), i),
            _layer_spec(_rows(sg_ln_b), i),
        ],
        out_specs=[row_spec(DA_QK_WIDTH), row_spec(DA_QK_WIDTH), row_spec(DA_V_WIDTH),
                   row_spec(SG_WIDTH), row_spec(SG_WIDTH)],
        out_shape=[
            jax.ShapeDtypeStruct((b, s, DA_QK_WIDTH), BF16),
            jax.ShapeDtypeStruct((b, s, DA_QK_WIDTH), BF16),
            jax.ShapeDtypeStruct((b, s, DA_V_WIDTH), BF16),
            jax.ShapeDtypeStruct((b, s, SG_WIDTH), F32),
            jax.ShapeDtypeStruct((b, s, SG_WIDTH), BF16),
        ],
        compiler_params=_params(2),
        name="even_in",
    )(x, _rows(norm_mix_g), ab_w_in, cos, sin, _rows(sg_ln_g), _rows(sg_ln_b))


def _pair_perm(j, n_tiles):
    return jnp.where(j < n_tiles // 2, 2 * j, 2 * (n_tiles - 1 - j) + 1)


def _attn_kernel(lam_ref, *refs, tq, n_tiles, lambda_init):
    q_refs = refs[:2 * ATTN_PAIRS]
    k_ref, v_ref, sg_ref, o_ref, qs_ref, m_ref, acc_ref, bias_ref = refs[2 * ATTN_PAIRS:]
    for ps in range(ATTN_PAIRS):
        _attn_pair(ATTN_PAIRS * pl.program_id(1) + ps, lam_ref, q_refs[2 * ps], q_refs[2 * ps + 1], k_ref,
                   v_ref, sg_ref, o_ref.at[pl.ds(ps * 2 * tq, 2 * tq)], qs_ref.at[ps], m_ref.at[ps],
                   acc_ref.at[ps], bias_ref.at[ps], tq=tq, n_tiles=n_tiles, lambda_init=lambda_init)


def _attn_pair(pair, lam_ref, qa_ref, qb_ref, k_ref, v_ref, sg_ref, o_ref, qs_ref, m_ref, acc_ref, bias_ref,
               *, tq, n_tiles, lambda_init):
    rows = 2 * tq
    width = 2 * tq
    last = n_tiles - 1
    nt = (((1,), (1,)), ((), ()))
    lane = lax.broadcasted_iota(jnp.int32, (tq, LANES), 1)
    ones = jnp.ones((width, LANES), BF16)

    def scores(sel, h, start):
        k = k_ref[pl.ds(start, width), h * LANES:(h + 1) * LANES]
        return lax.dot_general(qs_ref[sel, h], k, nt, preferred_element_type=F32)

    def values(h, start):
        return jnp.concatenate([v_ref[pl.ds(start, width), h * LANES:(h + 1) * LANES], ones], axis=1)

    def spread(m):
        return jnp.concatenate([m] * (width // LANES), axis=1)

    for t, tile in enumerate((pair, last - pair)):
        blk = tile // 2
        start = pl.multiple_of(blk * width, width)
        r = lax.broadcasted_iota(jnp.int32, (tq, width), 0)
        c = lax.broadcasted_iota(jnp.int32, (tq, width), 1)
        bias = jnp.where(c <= r + (tile * tq - blk * width), 0.0, -jnp.inf).astype(F32)
        bias_ref[t] = bias
        for h in range(DA_HEADS):
            qh = (qa_ref, qb_ref)[t][:, h * LANES:(h + 1) * LANES]
            zero = jnp.zeros_like(qh)
            qs_ref[t, h, 0:tq, :] = jnp.where(lane < DA_HEAD_DIM, qh, zero)
            qs_ref[t, h, tq:rows, :] = jnp.where(lane >= DA_HEAD_DIM, qh, zero)
            s = scores(t, h, start) + jnp.concatenate([bias_ref[t]] * 2, axis=0)
            m = jnp.broadcast_to(jnp.max(s, axis=-1, keepdims=True), (rows, LANES))
            p = jnp.exp2(s - spread(m)).astype(BF16)
            acc_ref[t, h] = jnp.dot(p, values(h, start), preferred_element_type=F32)
            m_ref[t, h] = m

    def full_block(sel, start):
        for h in range(DA_HEADS):
            s = scores(sel, h, start)
            m_old = m_ref[sel, h]
            m_new = jnp.maximum(m_old, jnp.max(s, axis=-1, keepdims=True))
            alpha = jnp.exp2(m_old - m_new)
            p = jnp.exp2(s - spread(m_new)).astype(BF16)
            pv = jnp.dot(p, values(h, start), preferred_element_type=F32)
            acc_ref[sel, h] = jnp.concatenate([alpha, alpha], axis=1) * acc_ref[sel, h] + pv
            m_ref[sel, h] = m_new

    lp = lam_ref[...]
    lam = (jnp.exp(jnp.sum(lp[0:1] * lp[1:2], axis=-1, keepdims=True))
           - jnp.exp(jnp.sum(lp[2:3] * lp[3:4], axis=-1, keepdims=True)) + lambda_init)

    def finish(t):
        for h in range(DA_HEADS):
            o1 = acc_ref[t, h, 0:tq, 0:LANES] / acc_ref[t, h, 0:tq, LANES:2 * LANES]
            o2 = acc_ref[t, h, tq:rows, 0:LANES] / acc_ref[t, h, tq:rows, LANES:2 * LANES]
            o = _rms(o1 - lam * o2, sg_ref[...]) * (1.0 - lambda_init)
            o_ref[t * tq:(t + 1) * tq, h * LANES:(h + 1) * LANES] = o.astype(BF16)

    full_a = pair // 2
    full_block(jnp.where(full_a > 0, 0, 1), 0)
    finish(0)
    for t in range(1, (n_tiles - 2) // 2):
        full_block(1, pl.multiple_of((t - full_a) * width, width))
    finish(1)


def _diff_attention(l, lam_params, q, k, v, subln_g, lambda_init):
    b, s, w = q.shape
    tq = TQ
    n_tiles = s // tq
    assert n_tiles % 4 == 0 and (n_tiles // 2 - 1) // 2 <= 1, "block schedule assumes n_tiles in (4, 8)"
    pairs = ATTN_PAIRS
    q_specs = []
    for ps in range(pairs):
        q_specs.append(pl.BlockSpec((None, tq, w), lambda i, j, ps=ps: (i, pairs * j + ps, 0)))
        q_specs.append(pl.BlockSpec((None, tq, w), lambda i, j, ps=ps: (i, n_tiles - 1 - (pairs * j + ps), 0)))
    return pl.pallas_call(
        functools.partial(_attn_kernel, tq=tq, n_tiles=n_tiles, lambda_init=lambda_init),
        grid=(b, n_tiles // (2 * pairs)),
        in_specs=[_layer_spec(lam_params, l // 2)] + q_specs + [
            pl.BlockSpec((None, s, w), lambda i, j: (i, 0, 0)),
            pl.BlockSpec((None, s, w), lambda i, j: (i, 0, 0)),
            _layer_spec(_rows(subln_g), l // 2),
        ],
        out_specs=pl.BlockSpec((None, pairs * 2 * tq, w), lambda i, j: (i, j, 0)),
        out_shape=jax.ShapeDtypeStruct((b, s, DA_V_WIDTH), BF16),
        scratch_shapes=[
            pltpu.VMEM((pairs, 2, DA_HEADS, 2 * tq, LANES), BF16),
            pltpu.VMEM((pairs, 2, DA_HEADS, 2 * tq, LANES), F32),
            pltpu.VMEM((pairs, 2, DA_HEADS, 2 * tq, 2 * LANES), F32),
            pltpu.VMEM((pairs, 2, tq, 2 * tq), F32),
        ],
        compiler_params=_params(2),
        name="diff_attn",
    )(lam_params, *([q] * (2 * pairs)), k, v, _rows(subln_g))


def _rope_tables(seq):
    dim = DA_HEAD_DIM
    inv = 1.0 / (ROPE_THETA ** (jnp.arange(0, dim, 2, dtype=F32) / dim))
    ang = jnp.arange(seq, dtype=F32)[:, None] * inv[None, :]
    ang = jnp.concatenate([ang, ang], axis=-1)
    cos = jnp.cos(ang)
    sin = jnp.sin(ang)
    sign = jnp.where(jnp.arange(dim) < dim // 2, -1.0, 1.0).astype(F32)
    reps = LANES // dim
    return jnp.tile(cos, (1, reps)), jnp.tile(sin * sign[None, :], (1, reps))


def kernel(x, norm_mix_g, norm_ffn_g, ab_w_in, ab_w_out, diff_lq1, diff_lk1, diff_lq2, diff_lk2,
           diff_subln_g, sg_ln_g, sg_ln_b, sg_w, sg_b, conv_w_in, conv_b_in, conv_dw_w, conv_dw_b,
           conv_ln_g, conv_ln_b, conv_w_out, conv_b_out, ffn_w_up, ffn_dw_w, ffn_dw_b, ffn_w_down,
           final_norm_g):
    seq = x.shape[1]
    cos, sin = _rope_tables(seq)
    ab_w_in = ab_w_in.astype(BF16)
    ab_w_out = ab_w_out.astype(BF16)
    conv_w_in = conv_w_in.astype(BF16)
    conv_w_out = conv_w_out.astype(BF16)
    ffn_w_up = ffn_w_up.astype(BF16)
    ffn_w_down = ffn_w_down.astype(BF16)
    lam_params = jnp.stack([diff_lq1, diff_lk1, diff_lq2, diff_lk2], axis=1)
    ffn_args = (norm_ffn_g, ffn_w_up, ffn_dw_w, ffn_dw_b, ffn_w_down, final_norm_g)
    for l in range(DEPTH):
        if l % 2 == 0:
            lambda_init = 0.8 - 0.6 * math.exp(-0.3 * l)
            q, k, v, u, z = _even_in(x, l, norm_mix_g, ab_w_in, cos, sin, sg_ln_g, sg_ln_b)
            o = _diff_attention(l, lam_params, q, k, v, diff_subln_g, lambda_init)
            x = _even_tail(o, u, z, x, l, sg_w, sg_b, ab_w_out, *ffn_args)
        else:
            x = _odd_layer(x, l, norm_mix_g, conv_w_in, conv_b_in, conv_dw_w, conv_dw_b, conv_ln_g,
                           conv_ln_b, conv_w_out, conv_b_out, *ffn_args)
    return x
```
